```python
import math
import jax, jax.numpy as jnp
from jax import lax
import numpy as np

D_MODEL = 2048
BATCH = 4
SEQ = 8192
DEPTH = 2

GRID_W = 64
HEAD_DIM = 128
N_HEADS = D_MODEL // HEAD_DIM
GA_KV_HEADS = N_HEADS // 4
Q_BLOCK = 128
ROPE_THETA = 10000.0
NA_WIN_ROWS = 8
NA_WIN_COLS = 16
FFN_DIM = ((8 * D_MODEL // 3 + 255) // 256) * 256
N_MIXERS = 2
N_LAYERS_A = (DEPTH + 1) // 2
N_LAYERS_B = DEPTH // 2
DEEPNORM_ALPHA = (2 * DEPTH) ** 0.25
DEEPNORM_BETA = (8 * DEPTH) ** -0.25
LN_EPS = 1e-5
QK_EPS = 1e-6

kernel_name = "interleaved_gqa_axial_rope_natten_macaron_deepnorm"


def layer_norm(x, g, b):
    xf = x.astype(jnp.float32)
    mu = jnp.mean(xf, axis=-1, keepdims=True)
    xc = xf - mu
    var = jnp.mean(xc * xc, axis=-1, keepdims=True)
    y = xc * lax.rsqrt(var + LN_EPS) * g.astype(jnp.float32) + b.astype(jnp.float32)
    return y.astype(x.dtype)


def deepnorm_residual(x, y, g, b):
    return layer_norm(DEEPNORM_ALPHA * x + y, g, b)


def swiglu(x, w_in, w_out):
    gate, up = jnp.split(x @ w_in, 2, axis=-1)
    return (jax.nn.silu(gate) * up) @ w_out


def rms_head_norm(x, g):
    xf = x.astype(jnp.float32)
    y = xf * lax.rsqrt(jnp.mean(xf * xf, axis=-1, keepdims=True) + QK_EPS) * g.astype(jnp.float32)
    return y


def axial_rope_tables(seq):
    t = jnp.arange(seq)
    row = (t // GRID_W).astype(jnp.float32)
    col = (t % GRID_W).astype(jnp.float32)
    axis_dim = HEAD_DIM // 2
    inv_freq = ROPE_THETA ** (-jnp.arange(0, axis_dim, 2, dtype=jnp.float32) / axis_dim)
    ang = jnp.stack([row[:, None] * inv_freq, col[:, None] * inv_freq], axis=1)
    return jnp.cos(ang), jnp.sin(ang)


def apply_axial_rope(xf, cos, sin, out_dtype):
    B, S, H, D = xf.shape
    nf = D // 4
    xr = xf.reshape(B, S, H, 2, 2, nf)
    x1, x2 = xr[..., 0, :], xr[..., 1, :]
    c = cos[None, :, None]
    s = sin[None, :, None]
    out = jnp.stack([x1 * c - x2 * s, x1 * s + x2 * c], axis=-2)
    return out.reshape(B, S, H, D).astype(out_dtype)


def grouped_query_attention(x, w_qkv, q_norm_g, k_norm_g, w_o):
    B, S, _ = x.shape
    qkv = x @ w_qkv
    q, k, v = jnp.split(qkv, [N_HEADS * HEAD_DIM, (N_HEADS + GA_KV_HEADS) * HEAD_DIM], axis=-1)
    q = q.reshape(B, S, N_HEADS, HEAD_DIM)
    k = k.reshape(B, S, GA_KV_HEADS, HEAD_DIM)
    v = v.reshape(B, S, GA_KV_HEADS, HEAD_DIM)
    cos, sin = axial_rope_tables(S)
    q = apply_axial_rope(rms_head_norm(q, q_norm_g), cos, sin, x.dtype)
    k = apply_axial_rope(rms_head_norm(k, k_norm_g), cos, sin, x.dtype)
    groups = N_HEADS // GA_KV_HEADS
    n_blocks = S // Q_BLOCK
    qb = q.reshape(B, n_blocks, Q_BLOCK, GA_KV_HEADS, groups, HEAD_DIM).transpose(1, 0, 2, 3, 4, 5)
    scale = HEAD_DIM ** -0.5

    def attend_block(q_blk):
        s = jnp.einsum('bqkgd,bskd->bkgqs', q_blk, k, preferred_element_type=jnp.float32) * scale
        p = jax.nn.softmax(s, axis=-1).astype(v.dtype)
        return jnp.einsum('bkgqs,bskd->bqkgd', p, v)

    o = lax.map(attend_block, qb)
    o = o.transpose(1, 0, 2, 3, 4, 5).reshape(B, S, N_HEADS * HEAD_DIM)
    return o @ w_o


def neighbourhood_attention(x, w_qkv, rpb, w_o):
    B, S, _ = x.shape
    rows = S // GRID_W
    kh = min(NA_WIN_ROWS, rows)
    qkv = (x @ w_qkv).reshape(B, rows, GRID_W, 3, N_HEADS, HEAD_DIM)
    q, k, v = qkv[:, :, :, 0], qkv[:, :, :, 1], qkv[:, :, :, 2]
    c = jnp.arange(GRID_W)
    col_start = jnp.clip(c - NA_WIN_COLS // 2, 0, GRID_W - NA_WIN_COLS)
    col_idx = col_start[:, None] + jnp.arange(NA_WIN_COLS)
    col_rel = col_idx - c[:, None] + (NA_WIN_COLS - 1)
    bias_col = rpb[:, :, col_rel]
    scale = HEAD_DIM ** -0.5

    def attend_row(r):
        row_start = jnp.clip(r - kh // 2, 0, rows - kh)
        row_rel = row_start + jnp.arange(kh) - r + (NA_WIN_ROWS - 1)
        bias = bias_col[:, row_rel].transpose(0, 2, 1, 3)
        k_rows = lax.dynamic_slice_in_dim(k, row_start, kh, axis=1)
        v_rows = lax.dynamic_slice_in_dim(v, row_start, kh, axis=1)
        k_win = k_rows[:, :, col_idx]
        v_win = v_rows[:, :, col_idx]
        q_row = lax.dynamic_index_in_dim(q, r, axis=1, keepdims=False)
        s = jnp.einsum('bchd,bicjhd->bhcij', q_row, k_win, preferred_element_type=jnp.float32) * scale
        s = s + bias.astype(jnp.float32)[None]
        p = jax.nn.softmax(s.reshape(B, N_HEADS, GRID_W, kh * NA_WIN_COLS), axis=-1)
        p = p.reshape(s.shape).astype(v.dtype)
        return jnp.einsum('bhcij,bicjhd->bchd', p, v_win)

    o = lax.map(attend_row, jnp.arange(rows))
    o = o.transpose(1, 0, 2, 3, 4).reshape(B, S, N_HEADS * HEAD_DIM)
    return o @ w_o


def setup_inputs(seed: int = 0) -> dict:
    key = jax.random.key(seed)
    ks = jax.random.split(key, 12)
    f32 = jnp.float32
    hd_all = N_HEADS * HEAD_DIM
    x = jax.random.normal(ks[0], (BATCH, SEQ, D_MODEL), f32)
    ln_g = 1.0 + 0.02 * jax.random.normal(ks[1], (DEPTH, 3, D_MODEL), f32)
    ln_b = 0.02 * jax.random.normal(ks[2], (DEPTH, 3, D_MODEL), f32)
    ffn_w_in = jax.random.normal(ks[3], (DEPTH, 2, D_MODEL, 2 * FFN_DIM), f32) * D_MODEL ** -0.5
    ffn_w_out = jax.random.normal(ks[4], (DEPTH, 2, FFN_DIM, D_MODEL), f32) * (FFN_DIM ** -0.5 * DEEPNORM_BETA)
    ga_w_qkv = jax.random.normal(ks[5], (N_LAYERS_A, D_MODEL, (N_HEADS + 2 * GA_KV_HEADS) * HEAD_DIM), f32) * D_MODEL ** -0.5
    ga_q_norm = 1.0 + 0.02 * jax.random.normal(ks[6], (N_LAYERS_A, HEAD_DIM), f32)
    ga_k_norm = 1.0 + 0.02 * jax.random.normal(ks[7], (N_LAYERS_A, HEAD_DIM), f32)
    ga_w_o = jax.random.normal(ks[8], (N_LAYERS_A, hd_all, D_MODEL), f32) * (hd_all ** -0.5 * DEEPNORM_BETA)
    na_w_qkv = jax.random.normal(ks[9], (N_LAYERS_B, D_MODEL, 3 * hd_all), f32) * D_MODEL ** -0.5
    na_rpb = 0.2 * jax.random.normal(ks[10], (N_LAYERS_B, N_HEADS, 2 * NA_WIN_ROWS - 1, 2 * NA_WIN_COLS - 1), f32)
    na_w_o = jax.random.normal(ks[11], (N_LAYERS_B, hd_all, D_MODEL), f32) * (hd_all ** -0.5 * DEEPNORM_BETA)
    return {"x": x, "ln_g": ln_g, "ln_b": ln_b, "ffn_w_in": ffn_w_in, "ffn_w_out": ffn_w_out,
            "ga_w_qkv": ga_w_qkv, "ga_q_norm": ga_q_norm, "ga_k_norm": ga_k_norm, "ga_w_o": ga_w_o,
            "na_w_qkv": na_w_qkv, "na_rpb": na_rpb, "na_w_o": na_w_o}


def reference(x, ln_g, ln_b, ffn_w_in, ffn_w_out, ga_w_qkv, ga_q_norm, ga_k_norm, ga_w_o,
              na_w_qkv, na_rpb, na_w_o):
    for i in range(DEPTH):
        g, b = ln_g[i], ln_b[i]
        x = deepnorm_residual(x, 0.5 * swiglu(x, ffn_w_in[i, 0], ffn_w_out[i, 0]), g[0], b[0])
        j = i // N_MIXERS
        if i % N_MIXERS == 0:
            y = grouped_query_attention(x, ga_w_qkv[j], ga_q_norm[j], ga_k_norm[j], ga_w_o[j])
        else:
            y = neighbourhood_attention(x, na_w_qkv[j], na_rpb[j], na_w_o[j])
        x = deepnorm_residual(x, y, g[1], b[1])
        x = deepnorm_residual(x, 0.5 * swiglu(x, ffn_w_in[i, 1], ffn_w_out[i, 1]), g[2], b[2])
    return x
```

```python
import functools
import math

import jax
import jax.numpy as jnp
import numpy as np
from jax import lax
from jax.experimental import pallas as pl
from jax.experimental.pallas import tpu as pltpu

F32 = jnp.float32
BF16 = jnp.bfloat16

GRID_W = 64
HEAD_DIM = 128
N_HEADS = 16
GA_KV_HEADS = 4
GA_GROUPS = N_HEADS // GA_KV_HEADS
ROPE_THETA = 10000.0
NA_WIN_ROWS = 8
NA_WIN_COLS = 16
DEPTH = 2
DEEPNORM_ALPHA = (2 * DEPTH) ** 0.25
LN_EPS = 1e-5
QK_EPS = 1e-6
ATTN_SCALE = HEAD_DIM ** -0.5
LOG2E = math.log2(math.e)
NEG_BIG = -1e30

VMEM_LIMIT = 56 * 1024 * 1024

NA_QROWS = 4
NA_KROWS = NA_QROWS + NA_WIN_ROWS


def _layer_norm(z, g, b):
    mu = jnp.mean(z, axis=-1, keepdims=True)
    zc = z - mu
    var = jnp.mean(zc * zc, axis=-1, keepdims=True)
    return zc * lax.rsqrt(var + LN_EPS) * g + b


def _ffn_ln_kernel(x_ref, wg_ref, wu_ref, wo_ref, g_ref, b_ref, o_ref, xb_ref, acc_ref):
    j = pl.program_id(1)

    @pl.when(j == 0)
    def _():
        xb_ref[...] = x_ref[...].astype(BF16)

    xb = xb_ref[...]
    gate = jnp.dot(xb, wg_ref[...], preferred_element_type=F32)
    up = jnp.dot(xb, wu_ref[...], preferred_element_type=F32)
    h = (gate * jax.nn.sigmoid(gate) * up).astype(BF16)
    part = jnp.dot(h, wo_ref[...], preferred_element_type=F32)

    @pl.when(j == 0)
    def _():
        acc_ref[...] = part

    @pl.when(j > 0)
    def _():
        acc_ref[...] += part

    @pl.when(j == pl.num_programs(1) - 1)
    def _():
        z = DEEPNORM_ALPHA * x_ref[...] + 0.5 * acc_ref[...]
        o_ref[...] = _layer_norm(z, g_ref[...], b_ref[...])


def _ffn_ln(x2, w_in, w_out, g, b, *, tm=512, tf=512):
    m, d = x2.shape
    f = w_out.shape[0]
    nf = f // tf
    return pl.pallas_call(
        _ffn_ln_kernel,
        out_shape=jax.ShapeDtypeStruct((m, d), F32),
        grid=(m // tm, nf),
        in_specs=[
            pl.BlockSpec((tm, d), lambda i, j: (i, 0)),
            pl.BlockSpec((d, tf), lambda i, j: (0, j)),
            pl.BlockSpec((d, tf), lambda i, j: (0, j + nf)),
            pl.BlockSpec((tf, d), lambda i, j: (j, 0)),
            pl.BlockSpec((1, d), lambda i, j: (0, 0)),
            pl.BlockSpec((1, d), lambda i, j: (0, 0)),
        ],
        out_specs=pl.BlockSpec((tm, d), lambda i, j: (i, 0)),
        scratch_shapes=[pltpu.VMEM((tm, d), BF16), pltpu.VMEM((tm, d), F32)],
        compiler_params=pltpu.CompilerParams(
            dimension_semantics=("parallel", "arbitrary"), vmem_limit_bytes=VMEM_LIMIT),
        name="ffn_ln",
    )(x2, w_in, w_in, w_out, g, b)


def _proj_ln_kernel(a_ref, x_ref, w_ref, g_ref, b_ref, o_ref):
    y = jnp.dot(a_ref[...], w_ref[...], preferred_element_type=F32)
    z = DEEPNORM_ALPHA * x_ref[...] + y
    o_ref[...] = _layer_norm(z, g_ref[...], b_ref[...])


def _proj_ln(a2, x2, w, g, b, *, tm=512):
    m, d = x2.shape
    k = a2.shape[1]
    return pl.pallas_call(
        _proj_ln_kernel,
        out_shape=jax.ShapeDtypeStruct((m, d), F32),
        grid=(m // tm,),
        in_specs=[
            pl.BlockSpec((tm, k), lambda i: (i, 0)),
            pl.BlockSpec((tm, d), lambda i: (i, 0)),
            pl.BlockSpec((k, d), lambda i: (0, 0)),
            pl.BlockSpec((1, d), lambda i: (0, 0)),
            pl.BlockSpec((1, d), lambda i: (0, 0)),
        ],
        out_specs=pl.BlockSpec((tm, d), lambda i: (i, 0)),
        compiler_params=pltpu.CompilerParams(
            dimension_semantics=("parallel",), vmem_limit_bytes=VMEM_LIMIT),
        name="proj_ln",
    )(a2, x2, w, g, b)


def _gqa_qkv_kernel(x_ref, w_ref, cos_ref, sa_ref, sb_ref, qg_ref, kg_ref, q_ref, k_ref, v_ref):
    xb = x_ref[0].astype(BF16)
    y = jnp.dot(xb, w_ref[...], preferred_element_type=F32)
    cos = cos_ref[...]
    sa = sa_ref[...]
    sb = sb_ref[...]

    def norm_rope(yh, gain):
        ms = jnp.mean(yh * yh, axis=-1, keepdims=True)
        yn = yh * lax.rsqrt(ms + QK_EPS) * gain
        return (yn * cos + pltpu.roll(yn, HEAD_DIM - 32, 1) * sa + pltpu.roll(yn, 32, 1) * sb)

    qg = qg_ref[...]
    kg = kg_ref[...]
    for h in range(N_HEADS):
        yh = y[:, h * HEAD_DIM:(h + 1) * HEAD_DIM]
        q_ref[0, h] = (norm_rope(yh, qg) * (ATTN_SCALE * LOG2E)).astype(BF16)
    base = N_HEADS * HEAD_DIM
    for h in range(GA_KV_HEADS):
        yh = y[:, base + h * HEAD_DIM: base + (h + 1) * HEAD_DIM]
        k_ref[0, h] = norm_rope(yh, kg).astype(BF16)
    base = (N_HEADS + GA_KV_HEADS) * HEAD_DIM
    for h in range(GA_KV_HEADS):
        v_ref[0, h] = y[:, base + h * HEAD_DIM: base + (h + 1) * HEAD_DIM].astype(BF16)


def _gqa_qkv(x3, w, cos, sa, sb, qg, kg, *, tm=512):
    bsz, s, d = x3.shape
    n = w.shape[1]
    hd = HEAD_DIM
    return pl.pallas_call(
        _gqa_qkv_kernel,
        out_shape=(
            jax.ShapeDtypeStruct((bsz, N_HEADS, s, hd), BF16),
            jax.ShapeDtypeStruct((bsz, GA_KV_HEADS, s, hd), BF16),
            jax.ShapeDtypeStruct((bsz, GA_KV_HEADS, s, hd), BF16),
        ),
        grid=(bsz, s // tm),
        in_specs=[
            pl.BlockSpec((1, tm, d), lambda b, i: (b, i, 0)),
            pl.BlockSpec((d, n), lambda b, i: (0, 0)),
            pl.BlockSpec((tm, hd), lambda b, i: (i, 0)),
            pl.BlockSpec((tm, hd), lambda b, i: (i, 0)),
            pl.BlockSpec((tm, hd), lambda b, i: (i, 0)),
            pl.BlockSpec((1, hd), lambda b, i: (0, 0)),
            pl.BlockSpec((1, hd), lambda b, i: (0, 0)),
        ],
        out_specs=(
            pl.BlockSpec((1, N_HEADS, tm, hd), lambda b, i: (b, 0, i, 0)),
            pl.BlockSpec((1, GA_KV_HEADS, tm, hd), lambda b, i: (b, 0, i, 0)),
            pl.BlockSpec((1, GA_KV_HEADS, tm, hd), lambda b, i: (b, 0, i, 0)),
        ),
        compiler_params=pltpu.CompilerParams(
            dimension_semantics=("parallel", "parallel"), vmem_limit_bytes=VMEM_LIMIT),
        name="gqa_qkv",
    )(x3, w, cos, sa, sb, qg, kg)


def _gqa_flash_kernel(q_ref, k_ref, v_ref, o_ref, m_ref, l_ref, acc_ref, *, tq, tk):
    mrows = GA_GROUPS * tq
    nk = k_ref.shape[2] // tk
    q = q_ref[0].reshape(mrows, HEAD_DIM)
    m_ref[...] = jnp.full(m_ref.shape, NEG_BIG, F32)
    l_ref[...] = jnp.zeros(l_ref.shape, F32)
    acc_ref[...] = jnp.zeros(acc_ref.shape, F32)

    def body(i, carry):
        off = pl.multiple_of(i * tk, tk)
        k = k_ref[0, 0, pl.ds(off, tk), :]
        v = v_ref[0, 0, pl.ds(off, tk), :]
        s = lax.dot_general(q, k, (((1,), (1,)), ((), ())), preferred_element_type=F32)
        m_prev = m_ref[...]
        m_new = jnp.maximum(m_prev, jnp.max(s, axis=1, keepdims=True))
        alpha = jnp.exp2(m_prev - m_new)
        p = jnp.exp2(s - jnp.tile(m_new, (1, tk // HEAD_DIM)))
        l_ref[...] = alpha * l_ref[...] + jnp.sum(p, axis=1, keepdims=True)
        acc_ref[...] = alpha * acc_ref[...] + jnp.dot(p.astype(BF16), v, preferred_element_type=F32)
        m_ref[...] = m_new
        return carry

    lax.fori_loop(0, nk, body, 0)
    out = acc_ref[...] / l_ref[...]
    for h in range(GA_GROUPS):
        o_ref[0, :, h * HEAD_DIM:(h + 1) * HEAD_DIM] = out[h * tq:(h + 1) * tq].astype(BF16)


def _gqa_flash(q, k, v, *, tq=256, tk=512):
    bsz, _, s, hd = q.shape
    mrows = GA_GROUPS * tq
    return pl.pallas_call(
        functools.partial(_gqa_flash_kernel, tq=tq, tk=tk),
        out_shape=jax.ShapeDtypeStruct((bsz, s, N_HEADS * hd), BF16),
        grid=(bsz, GA_KV_HEADS, s // tq),
        in_specs=[
            pl.BlockSpec((1, GA_GROUPS, tq, hd), lambda b, g, i: (b, g, i, 0)),
            pl.BlockSpec((1, 1, s, hd), lambda b, g, i: (b, g, 0, 0)),
            pl.BlockSpec((1, 1, s, hd), lambda b, g, i: (b, g, 0, 0)),
        ],
        out_specs=pl.BlockSpec((1, tq, GA_GROUPS * hd), lambda b, g, i: (b, i, g)),
        scratch_shapes=[pltpu.VMEM((mrows, hd), F32), pltpu.VMEM((mrows, hd), F32),
                        pltpu.VMEM((mrows, hd), F32)],
        compiler_params=pltpu.CompilerParams(
            dimension_semantics=("parallel", "parallel", "parallel"), vmem_limit_bytes=VMEM_LIMIT),
        name="gqa_flash",
    )(q, k, v)


def _na_qkv_kernel(x_ref, w_ref, o_ref):
    xb = x_ref[0].astype(BF16)
    y = jnp.dot(xb, w_ref[...], preferred_element_type=F32)
    for t in range(o_ref.shape[1]):
        o_ref[0, t] = y[:, t * HEAD_DIM:(t + 1) * HEAD_DIM].astype(BF16)


def _na_qkv(x3, w, *, tm=512, tn=1024):
    bsz, s, d = x3.shape
    n = w.shape[1]
    hd = HEAD_DIM
    return pl.pallas_call(
        _na_qkv_kernel,
        out_shape=jax.ShapeDtypeStruct((bsz, n // hd, s, hd), BF16),
        grid=(bsz, s // tm, n // tn),
        in_specs=[
            pl.BlockSpec((1, tm, d), lambda b, i, j: (b, i, 0)),
            pl.BlockSpec((d, tn), lambda b, i, j: (0, j)),
        ],
        out_specs=pl.BlockSpec((1, tn // hd, tm, hd), lambda b, i, j: (b, j, i, 0)),
        compiler_params=pltpu.CompilerParams(
            dimension_semantics=("parallel", "parallel", "arbitrary"), vmem_limit_bytes=VMEM_LIMIT),
        name="na_qkv",
    )(x3, w)


def _na_attn_kernel(q_ref, k_ref, v_ref, bias_ref, o_ref, *, rows):
    nq = NA_QROWS * GRID_W
    nkeys = NA_KROWS * GRID_W
    nblk = rows // NA_QROWS

    def body(rb, carry):
        qoff = pl.multiple_of(rb * nq, nq)
        start = jnp.clip(rb * NA_QROWS - NA_WIN_ROWS // 2, 0, rows - NA_KROWS)
        koff = pl.multiple_of(start * GRID_W, GRID_W)
        cls = jnp.where(rb == 0, 0, jnp.where(rb == nblk - 1, 2, 1))
        q = q_ref[0, 0, pl.ds(qoff, nq), :]
        k = k_ref[0, 0, pl.ds(koff, nkeys), :]
        v = v_ref[0, 0, pl.ds(koff, nkeys), :]
        s = lax.dot_general(q, k, (((1,), (1,)), ((), ())), preferred_element_type=F32)
        s = s * ATTN_SCALE + bias_ref[cls, 0]
        m = jnp.max(s, axis=1, keepdims=True)
        p = jnp.exp(s - m)
        l = jnp.sum(p, axis=1, keepdims=True)
        o = jnp.dot(p.astype(BF16), v, preferred_element_type=F32) / l
        o_ref[0, pl.ds(qoff, nq), :] = o.astype(BF16)
        return carry

    lax.fori_loop(0, nblk, body, 0)


def _na_attn(qkv, bias):
    bsz, _, s, hd = qkv.shape
    rows = s // GRID_W
    nq = NA_QROWS * GRID_W
    nkeys = NA_KROWS * GRID_W
    return pl.pallas_call(
        functools.partial(_na_attn_kernel, rows=rows),
        out_shape=jax.ShapeDtypeStruct((bsz, s, N_HEADS * hd), BF16),
        grid=(bsz, N_HEADS),
        in_specs=[
            pl.BlockSpec((1, 1, s, hd), lambda b, h: (b, h, 0, 0)),
            pl.BlockSpec((1, 1, s, hd), lambda b, h: (b, N_HEADS + h, 0, 0)),
            pl.BlockSpec((1, 1, s, hd), lambda b, h: (b, 2 * N_HEADS + h, 0, 0)),
            pl.BlockSpec((3, 1, nq, nkeys), lambda b, h: (0, h, 0, 0)),
        ],
        out_specs=pl.BlockSpec((1, s, hd), lambda b, h: (b, 0, h)),
        compiler_params=pltpu.CompilerParams(
            dimension_semantics=("parallel", "parallel"), vmem_limit_bytes=VMEM_LIMIT),
        name="na_attn",
    )(qkv, qkv, qkv, bias)


def _na_bias_table(rpb, rows):
    h = rpb.shape[0]
    c = np.arange(GRID_W)
    col_start = np.clip(c - NA_WIN_COLS // 2, 0, GRID_W - NA_WIN_COLS)
    kc = np.arange(GRID_W)
    col_rel = kc[None, :] - c[:, None] + (NA_WIN_COLS - 1)
    col_ok = (kc[None, :] >= col_start[:, None]) & (kc[None, :] < col_start[:, None] + NA_WIN_COLS)
    col_rel = np.clip(col_rel, 0, 2 * NA_WIN_COLS - 2)
    t = jnp.where(jnp.asarray(col_ok)[None, None], rpb[:, :, jnp.asarray(col_rel)], NEG_BIG)
    neg_idx = 2 * NA_WIN_ROWS - 1
    t = jnp.concatenate([t, jnp.full((h, 1, GRID_W, GRID_W), NEG_BIG, F32)], axis=1)

    nblk = rows // NA_QROWS
    idx = np.full((3, NA_QROWS, NA_KROWS), neg_idx, np.int32)
    for cls, rb in enumerate((0, 1, nblk - 1)):
        r0 = rb * NA_QROWS
        start = int(np.clip(r0 - NA_WIN_ROWS // 2, 0, rows - NA_KROWS))
        for ri in range(NA_QROWS):
            r = r0 + ri
            row_start = int(np.clip(r - NA_WIN_ROWS // 2, 0, rows - NA_WIN_ROWS))
            for ki in range(NA_KROWS):
                kr = start + ki
                if row_start <= kr < row_start + NA_WIN_ROWS:
                    idx[cls, ri, ki] = kr - r + (NA_WIN_ROWS - 1)
    blocks = jnp.take(t, jnp.asarray(idx.reshape(-1)), axis=1)
    blocks = blocks.reshape(h, 3, NA_QROWS, NA_KROWS, GRID_W, GRID_W)
    blocks = blocks.transpose(1, 0, 2, 4, 3, 5)
    return blocks.reshape(3, h, NA_QROWS * GRID_W, NA_KROWS * GRID_W)


def _rope_tables(seq):
    t = jnp.arange(seq)
    row = (t // GRID_W).astype(F32)
    col = (t % GRID_W).astype(F32)
    axis_dim = HEAD_DIM // 2
    inv_freq = ROPE_THETA ** (-jnp.arange(0, axis_dim, 2, dtype=F32) / axis_dim)
    ar = row[:, None] * inv_freq
    ac = col[:, None] * inv_freq
    z = jnp.zeros_like(ar)
    cos = jnp.concatenate([jnp.cos(ar), jnp.cos(ar), jnp.cos(ac), jnp.cos(ac)], axis=1)
    sa = jnp.concatenate([-jnp.sin(ar), z, -jnp.sin(ac), z], axis=1)
    sb = jnp.concatenate([z, jnp.sin(ar), z, jnp.sin(ac)], axis=1)
    return cos, sa, sb


def kernel(x, ln_g, ln_b, ffn_w_in, ffn_w_out, ga_w_qkv, ga_q_norm, ga_k_norm, ga_w_o,
           na_w_qkv, na_rpb, na_w_o):
    bsz, s, d = x.shape
    m = bsz * s
    rows = s // GRID_W
    cos, sa, sb = _rope_tables(s)

    def ln_p(i, k):
        return ln_g[i, k][None, :], ln_b[i, k][None, :]

    def ffn(h, i, k):
        return _ffn_ln(h, ffn_w_in[i, k].astype(BF16), ffn_w_out[i, k].astype(BF16), *ln_p(i, 2 * k))

    h = x.reshape(m, d)
    for i in range(DEPTH):
        h = ffn(h, i, 0)
        j = i // 2
        if i % 2 == 0:
            q, k, v = _gqa_qkv(h.reshape(bsz, s, d), ga_w_qkv[j].astype(BF16), cos, sa, sb,
                               ga_q_norm[j][None, :], ga_k_norm[j][None, :])
            o = _gqa_flash(q, k, v)
            w_o = ga_w_o[j]
        else:
            qkv = _na_qkv(h.reshape(bsz, s, d), na_w_qkv[j].astype(BF16))
            o = _na_attn(qkv, _na_bias_table(na_rpb[j], rows))
            w_o = na_w_o[j]
        h = _proj_ln(o.reshape(m, d), h, w_o.astype(BF16), *ln_p(i, 1))
        h = ffn(h, i, 1)
    return h.reshape(bsz, s, d)
```

```python
import functools
import math

import jax
import jax.numpy as jnp
import numpy as np
from jax import lax
from jax.experimental import pallas as pl
from jax.experimental.pallas import tpu as pltpu

F32 = jnp.float32
BF16 = jnp.bfloat16

GRID_W = 64
HEAD_DIM = 128
N_HEADS = 16
GA_KV_HEADS = 4
GA_GROUPS = N_HEADS // GA_KV_HEADS
ROPE_THETA = 10000.0
NA_WIN_ROWS = 8
NA_WIN_COLS = 16
DEPTH = 2
DEEPNORM_ALPHA = (2 * DEPTH) ** 0.25
LN_EPS = 1e-5
QK_EPS = 1e-6
ATTN_SCALE = HEAD_DIM ** -0.5
LOG2E = math.log2(math.e)
NEG_BIG = -1e30
VT_ROWS = HEAD_DIM + 16
NORM_SLACK = 1.0 + 2.0 ** -6
STABILISER_MAX = 48.0
FLASH_UNROLL = 2

VMEM_LIMIT = 56 * 1024 * 1024

NA_QROWS = 4
NA_KROWS = NA_QROWS + NA_WIN_ROWS


def _layer_norm(z, g, b):
    mu = jnp.mean(z, axis=-1, keepdims=True)
    zc = z - mu
    var = jnp.mean(zc * zc, axis=-1, keepdims=True)
    return zc * lax.rsqrt(var + LN_EPS) * g + b


def _ffn_ln_kernel(x_ref, wg_ref, wu_ref, wo_ref, g_ref, b_ref, o_ref, xb_ref):
    j = pl.program_id(1)

    @pl.when(j == 0)
    def _():
        xb_ref[...] = x_ref[...].astype(BF16)
        o_ref[...] = jnp.zeros(o_ref.shape, F32)

    xb = xb_ref[...]
    gate = jnp.dot(xb, wg_ref[...], preferred_element_type=F32)
    up = jnp.dot(xb, wu_ref[...], preferred_element_type=F32)
    h = (gate * jax.nn.sigmoid(gate) * up).astype(BF16)
    o_ref[...] += jnp.dot(h, wo_ref[...], preferred_element_type=F32)

    @pl.when(j == pl.num_programs(1) - 1)
    def _():
        z = DEEPNORM_ALPHA * x_ref[...] + 0.5 * o_ref[...]
        o_ref[...] = _layer_norm(z, g_ref[...], b_ref[...])


def _ffn_ln(x2, w_in, w_out, g, b, *, tm=512, tf=512):
    m, d = x2.shape
    f = w_out.shape[0]
    nf = f // tf
    return pl.pallas_call(
        _ffn_ln_kernel,
        out_shape=jax.ShapeDtypeStruct((m, d), F32),
        grid=(m // tm, nf),
        in_specs=[
            pl.BlockSpec((tm, d), lambda i, j: (i, 0)),
            pl.BlockSpec((d, tf), lambda i, j: (0, j)),
            pl.BlockSpec((d, tf), lambda i, j: (0, j + nf)),
            pl.BlockSpec((tf, d), lambda i, j: (j, 0)),
            pl.BlockSpec((1, d), lambda i, j: (0, 0)),
            pl.BlockSpec((1, d), lambda i, j: (0, 0)),
        ],
        out_specs=pl.BlockSpec((tm, d), lambda i, j: (i, 0)),
        scratch_shapes=[pltpu.VMEM((tm, d), BF16)],
        compiler_params=pltpu.CompilerParams(
            dimension_semantics=("parallel", "arbitrary"), vmem_limit_bytes=VMEM_LIMIT),
        name="ffn_ln",
    )(x2, w_in, w_in, w_out, g, b)


def _proj_ln_kernel(a_ref, x_ref, w_ref, g_ref, b_ref, o_ref):
    y = jnp.dot(a_ref[...], w_ref[...], preferred_element_type=F32)
    z = DEEPNORM_ALPHA * x_ref[...] + y
    o_ref[...] = _layer_norm(z, g_ref[...], b_ref[...])


def _proj_ln(a2, x2, w, g, b, *, tm=512):
    m, d = x2.shape
    k = a2.shape[1]
    return pl.pallas_call(
        _proj_ln_kernel,
        out_shape=jax.ShapeDtypeStruct((m, d), F32),
        grid=(m // tm,),
        in_specs=[
            pl.BlockSpec((tm, k), lambda i: (i, 0)),
            pl.BlockSpec((tm, d), lambda i: (i, 0)),
            pl.BlockSpec((k, d), lambda i: (0, 0)),
            pl.BlockSpec((1, d), lambda i: (0, 0)),
            pl.BlockSpec((1, d), lambda i: (0, 0)),
        ],
        out_specs=pl.BlockSpec((tm, d), lambda i: (i, 0)),
        compiler_params=pltpu.CompilerParams(
            dimension_semantics=("parallel",), vmem_limit_bytes=VMEM_LIMIT),
        name="proj_ln",
    )(a2, x2, w, g, b)


def _gqa_qkv_kernel(x_ref, w_ref, cos_ref, sa_ref, sb_ref, qg_ref, kg_ref,
                    q_ref, k_ref, vt_ref, kmax_ref):
    tm = x_ref.shape[1]
    xb = x_ref[0].astype(BF16)
    y = jnp.dot(xb, w_ref[...], preferred_element_type=F32)
    cos = cos_ref[...]
    sa = sa_ref[...]
    sb = sb_ref[...]
    lane0 = lax.broadcasted_iota(jnp.int32, (tm, HEAD_DIM), 1) == 0

    def norm_rope(yh, gain):
        ms = jnp.mean(yh * yh, axis=-1, keepdims=True)
        yn = yh * lax.rsqrt(ms + QK_EPS) * gain
        return (yn * cos + pltpu.roll(yn, HEAD_DIM - 32, 1) * sa + pltpu.roll(yn, 32, 1) * sb)

    qg = qg_ref[...]
    kg = kg_ref[...]
    for h in range(N_HEADS):
        yh = y[:, h * HEAD_DIM:(h + 1) * HEAD_DIM]
        qs = norm_rope(yh, qg) * (ATTN_SCALE * LOG2E)
        q_ref[0, h, :, :HEAD_DIM] = qs.astype(BF16)
        qn = jnp.sqrt(jnp.sum(qs * qs, axis=-1, keepdims=True)) * NORM_SLACK
        q_ref[0, h, :, HEAD_DIM:] = jnp.where(lane0, qn, 0.0).astype(BF16)
    base = N_HEADS * HEAD_DIM
    for h in range(GA_KV_HEADS):
        yh = y[:, base + h * HEAD_DIM: base + (h + 1) * HEAD_DIM]
        kk = norm_rope(yh, kg)
        k_ref[0, h, :, :HEAD_DIM] = kk.astype(BF16)
        k_ref[0, h, :, HEAD_DIM:] = jnp.where(lane0, 1.0, 0.0).astype(BF16)
        ksq = jnp.max(jnp.sum(kk * kk, axis=-1, keepdims=True), axis=0, keepdims=True)
        kmax_ref[0, h, 0] = jnp.broadcast_to(jnp.sqrt(ksq) * NORM_SLACK, (8, HEAD_DIM))
    base = (N_HEADS + GA_KV_HEADS) * HEAD_DIM
    for h in range(GA_KV_HEADS):
        vt_ref[0, h, 0, :HEAD_DIM, :] = y[:, base + h * HEAD_DIM: base + (h + 1) * HEAD_DIM].T.astype(BF16)
        vt_ref[0, h, 0, HEAD_DIM:, :] = jnp.ones((VT_ROWS - HEAD_DIM, tm), BF16)


def _gqa_qkv(x3, w, cos, sa, sb, qg, kg, *, tm=512):
    bsz, s, d = x3.shape
    n = w.shape[1]
    hd = HEAD_DIM
    return pl.pallas_call(
        _gqa_qkv_kernel,
        out_shape=(
            jax.ShapeDtypeStruct((bsz, N_HEADS, s, 2 * hd), BF16),
            jax.ShapeDtypeStruct((bsz, GA_KV_HEADS, s, 2 * hd), BF16),
            jax.ShapeDtypeStruct((bsz, GA_KV_HEADS, s // tm, VT_ROWS, tm), BF16),
            jax.ShapeDtypeStruct((bsz, GA_KV_HEADS, s // tm, 8, hd), F32),
        ),
        grid=(bsz, s // tm),
        in_specs=[
            pl.BlockSpec((1, tm, d), lambda b, i: (b, i, 0)),
            pl.BlockSpec((d, n), lambda b, i: (0, 0)),
            pl.BlockSpec((tm, hd), lambda b, i: (i, 0)),
            pl.BlockSpec((tm, hd), lambda b, i: (i, 0)),
            pl.BlockSpec((tm, hd), lambda b, i: (i, 0)),
            pl.BlockSpec((1, hd), lambda b, i: (0, 0)),
            pl.BlockSpec((1, hd), lambda b, i: (0, 0)),
        ],
        out_specs=(
            pl.BlockSpec((1, N_HEADS, tm, 2 * hd), lambda b, i: (b, 0, i, 0)),
            pl.BlockSpec((1, GA_KV_HEADS, tm, 2 * hd), lambda b, i: (b, 0, i, 0)),
            pl.BlockSpec((1, GA_KV_HEADS, 1, VT_ROWS, tm), lambda b, i: (b, 0, i, 0, 0)),
            pl.BlockSpec((1, GA_KV_HEADS, 1, 8, hd), lambda b, i: (b, 0, i, 0, 0)),
        ),
        compiler_params=pltpu.CompilerParams(
            dimension_semantics=("parallel", "parallel"), vmem_limit_bytes=VMEM_LIMIT),
        name="gqa_qkv",
    )(x3, w, cos, sa, sb, qg, kg)


def _gqa_flash_kernel(q_ref, k_ref, vt_ref, kmax_ref, o_ref, q2_ref, mx_ref, acc_ref, st_ref,
                      *, tq, tk):
    hd = HEAD_DIM
    nk = k_ref.shape[2] // tk
    kmax = jnp.max(kmax_ref[0, 0], axis=0)[0:1, :]
    bound = q_ref[0, :, :, hd:].astype(F32) * kmax
    q2_ref[:, :, :hd] = q_ref[0, :, :, :hd]
    q2_ref[:, :, hd:] = (-bound).astype(BF16)

    @pl.when(jnp.max(bound) > STABILISER_MAX)
    def _():
        lane0 = lax.broadcasted_iota(jnp.int32, (tq, hd), 1) == 0
        for h in range(GA_GROUPS):
            mx_ref[...] = jnp.full(mx_ref.shape, NEG_BIG, F32)

            def max_body(i, carry):
                off = pl.multiple_of(i * tk, tk)
                k = k_ref[0, 0, pl.ds(off, tk), :hd]
                s = lax.dot_general(q_ref[0, h, :, :hd], k, (((1,), (1,)), ((), ())),
                                    preferred_element_type=F32)
                mx_ref[...] = jnp.maximum(mx_ref[...], jnp.max(s, axis=1, keepdims=True))
                return carry

            lax.fori_loop(0, nk, max_body, 0)
            q2_ref[h, :, hd:] = jnp.where(lane0, -mx_ref[...], 0.0).astype(BF16)

    acc_ref[...] = jnp.zeros(acc_ref.shape, F32)

    def scores(i, h):
        off = pl.multiple_of(i * tk, tk)
        return lax.dot_general(k_ref[0, 0, pl.ds(off, tk), :], q2_ref[h], (((1,), (1,)), ((), ())),
                               preferred_element_type=F32)

    st_ref[...] = scores(0, 0)

    def body(it, carry):
        st = st_ref[...]
        for u in range(FLASH_UNROLL):
            i = it * FLASH_UNROLL + u
            vt = vt_ref[0, 0, i]
            for h in range(GA_GROUPS):
                if h + 1 < GA_GROUPS:
                    st_next = scores(i, h + 1)
                else:
                    st_next = scores(jnp.minimum(i + 1, nk - 1), 0)
                p = jnp.exp2(st).astype(BF16)
                acc_ref[h] += jnp.dot(vt, p, preferred_element_type=F32)
                st = st_next
        st_ref[...] = st
        return carry

    lax.fori_loop(0, nk // FLASH_UNROLL, body, 0)
    for h in range(GA_GROUPS):
        out_t = acc_ref[h, :hd, :] / acc_ref[h, hd:hd + 1, :]
        o_ref[0, :, h * hd:(h + 1) * hd] = out_t.T.astype(BF16)


def _gqa_flash(q, k, vt, kmax, *, tq=512):
    bsz, _, s, hd2 = q.shape
    hd = hd2 // 2
    nk, tk = vt.shape[2], vt.shape[4]
    return pl.pallas_call(
        functools.partial(_gqa_flash_kernel, tq=tq, tk=tk),
        out_shape=jax.ShapeDtypeStruct((bsz, s, N_HEADS * hd), BF16),
        grid=(bsz, GA_KV_HEADS, s // tq),
        in_specs=[
            pl.BlockSpec((1, GA_GROUPS, tq, hd2), lambda b, g, i: (b, g, i, 0)),
            pl.BlockSpec((1, 1, s, hd2), lambda b, g, i: (b, g, 0, 0)),
            pl.BlockSpec((1, 1, nk, VT_ROWS, tk), lambda b, g, i: (b, g, 0, 0, 0)),
            pl.BlockSpec((1, 1, nk, 8, hd), lambda b, g, i: (b, g, 0, 0, 0)),
        ],
        out_specs=pl.BlockSpec((1, tq, GA_GROUPS * hd), lambda b, g, i: (b, i, g)),
        scratch_shapes=[pltpu.VMEM((GA_GROUPS, tq, hd2), BF16), pltpu.VMEM((tq, hd), F32),
                        pltpu.VMEM((GA_GROUPS, VT_ROWS, tq), F32), pltpu.VMEM((tk, tq), F32)],
        compiler_params=pltpu.CompilerParams(
            dimension_semantics=("parallel", "parallel", "parallel"), vmem_limit_bytes=VMEM_LIMIT),
        name="gqa_flash",
    )(q, k, vt, kmax)


def _na_qkv_kernel(x_ref, w_ref, o_ref):
    xb = x_ref[0].astype(BF16)
    y = jnp.dot(xb, w_ref[...], preferred_element_type=F32)
    for t in range(o_ref.shape[1]):
        o_ref[0, t] = y[:, t * HEAD_DIM:(t + 1) * HEAD_DIM].astype(BF16)


def _na_qkv(x3, w, *, tm=512, tn=1024):
    bsz, s, d = x3.shape
    n = w.shape[1]
    hd = HEAD_DIM
    return pl.pallas_call(
        _na_qkv_kernel,
        out_shape=jax.ShapeDtypeStruct((bsz, n // hd, s, hd), BF16),
        grid=(bsz, s // tm, n // tn),
        in_specs=[
            pl.BlockSpec((1, tm, d), lambda b, i, j: (b, i, 0)),
            pl.BlockSpec((d, tn), lambda b, i, j: (0, j)),
        ],
        out_specs=pl.BlockSpec((1, tn // hd, tm, hd), lambda b, i, j: (b, j, i, 0)),
        compiler_params=pltpu.CompilerParams(
            dimension_semantics=("parallel", "parallel", "arbitrary"), vmem_limit_bytes=VMEM_LIMIT),
        name="na_qkv",
    )(x3, w)


def _na_attn_kernel(q_ref, k_ref, v_ref, bias_ref, o_ref, *, rows):
    nq = NA_QROWS * GRID_W
    nkeys = NA_KROWS * GRID_W
    nblk = rows // NA_QROWS

    def body(rb, carry):
        qoff = pl.multiple_of(rb * nq, nq)
        start = jnp.clip(rb * NA_QROWS - NA_WIN_ROWS // 2, 0, rows - NA_KROWS)
        koff = pl.multiple_of(start * GRID_W, GRID_W)
        cls = jnp.where(rb == 0, 0, jnp.where(rb == nblk - 1, 2, 1))
        q = q_ref[0, 0, pl.ds(qoff, nq), :]
        k = k_ref[0, 0, pl.ds(koff, nkeys), :]
        v = v_ref[0, 0, pl.ds(koff, nkeys), :]
        s = lax.dot_general(q, k, (((1,), (1,)), ((), ())), preferred_element_type=F32)
        s = s * ATTN_SCALE + bias_ref[cls, 0]
        m = jnp.max(s, axis=1, keepdims=True)
        p = jnp.exp(s - m)
        l = jnp.sum(p, axis=1, keepdims=True)
        o = jnp.dot(p.astype(BF16), v, preferred_element_type=F32) / l
        o_ref[0, pl.ds(qoff, nq), :] = o.astype(BF16)
        return carry

    lax.fori_loop(0, nblk, body, 0)


def _na_attn(qkv, bias):
    bsz, _, s, hd = qkv.shape
    rows = s // GRID_W
    nq = NA_QROWS * GRID_W
    nkeys = NA_KROWS * GRID_W
    return pl.pallas_call(
        functools.partial(_na_attn_kernel, rows=rows),
        out_shape=jax.ShapeDtypeStruct((bsz, s, N_HEADS * hd), BF16),
        grid=(bsz, N_HEADS),
        in_specs=[
            pl.BlockSpec((1, 1, s, hd), lambda b, h: (b, h, 0, 0)),
            pl.BlockSpec((1, 1, s, hd), lambda b, h: (b, N_HEADS + h, 0, 0)),
            pl.BlockSpec((1, 1, s, hd), lambda b, h: (b, 2 * N_HEADS + h, 0, 0)),
            pl.BlockSpec((3, 1, nq, nkeys), lambda b, h: (0, h, 0, 0)),
        ],
        out_specs=pl.BlockSpec((1, s, hd), lambda b, h: (b, 0, h)),
        compiler_params=pltpu.CompilerParams(
            dimension_semantics=("parallel", "parallel"), vmem_limit_bytes=VMEM_LIMIT),
        name="na_attn",
    )(qkv, qkv, qkv, bias)


def _na_bias_table(rpb, rows):
    h = rpb.shape[0]
    c = np.arange(GRID_W)
    col_start = np.clip(c - NA_WIN_COLS // 2, 0, GRID_W - NA_WIN_COLS)
    kc = np.arange(GRID_W)
    col_rel = kc[None, :] - c[:, None] + (NA_WIN_COLS - 1)
    col_ok = (kc[None, :] >= col_start[:, None]) & (kc[None, :] < col_start[:, None] + NA_WIN_COLS)
    col_rel = np.clip(col_rel, 0, 2 * NA_WIN_COLS - 2)
    t = jnp.where(jnp.asarray(col_ok)[None, None], rpb[:, :, jnp.asarray(col_rel)], NEG_BIG)
    neg_idx = 2 * NA_WIN_ROWS - 1
    t = jnp.concatenate([t, jnp.full((h, 1, GRID_W, GRID_W), NEG_BIG, F32)], axis=1)

    nblk = rows // NA_QROWS
    idx = np.full((3, NA_QROWS, NA_KROWS), neg_idx, np.int32)
    for cls, rb in enumerate((0, 1, nblk - 1)):
        r0 = rb * NA_QROWS
        start = int(np.clip(r0 - NA_WIN_ROWS // 2, 0, rows - NA_KROWS))
        for ri in range(NA_QROWS):
            r = r0 + ri
            row_start = int(np.clip(r - NA_WIN_ROWS // 2, 0, rows - NA_WIN_ROWS))
            for ki in range(NA_KROWS):
                kr = start + ki
                if row_start <= kr < row_start + NA_WIN_ROWS:
                    idx[cls, ri, ki] = kr - r + (NA_WIN_ROWS - 1)
    blocks = jnp.take(t, jnp.asarray(idx.reshape(-1)), axis=1)
    blocks = blocks.reshape(h, 3, NA_QROWS, NA_KROWS, GRID_W, GRID_W)
    blocks = blocks.transpose(1, 0, 2, 4, 3, 5)
    return blocks.reshape(3, h, NA_QROWS * GRID_W, NA_KROWS * GRID_W)


def _rope_tables(seq):
    t = jnp.arange(seq)
    row = (t // GRID_W).astype(F32)
    col = (t % GRID_W).astype(F32)
    axis_dim = HEAD_DIM // 2
    inv_freq = ROPE_THETA ** (-jnp.arange(0, axis_dim, 2, dtype=F32) / axis_dim)
    ar = row[:, None] * inv_freq
    ac = col[:, None] * inv_freq
    z = jnp.zeros_like(ar)
    cos = jnp.concatenate([jnp.cos(ar), jnp.cos(ar), jnp.cos(ac), jnp.cos(ac)], axis=1)
    sa = jnp.concatenate([-jnp.sin(ar), z, -jnp.sin(ac), z], axis=1)
    sb = jnp.concatenate([z, jnp.sin(ar), z, jnp.sin(ac)], axis=1)
    return cos, sa, sb


def kernel(x, ln_g, ln_b, ffn_w_in, ffn_w_out, ga_w_qkv, ga_q_norm, ga_k_norm, ga_w_o,
           na_w_qkv, na_rpb, na_w_o):
    bsz, s, d = x.shape
    m = bsz * s
    rows = s // GRID_W
    cos, sa, sb = _rope_tables(s)

    def ln_p(i, k):
        return ln_g[i, k][None, :], ln_b[i, k][None, :]

    def ffn(h, i, k):
        return _ffn_ln(h, ffn_w_in[i, k].astype(BF16), ffn_w_out[i, k].astype(BF16), *ln_p(i, 2 * k))

    h = x.reshape(m, d)
    for i in range(DEPTH):
        h = ffn(h, i, 0)
        j = i // 2
        if i % 2 == 0:
            q, k, vt, kmax = _gqa_qkv(h.reshape(bsz, s, d), ga_w_qkv[j].astype(BF16), cos, sa, sb,
                                      ga_q_norm[j][None, :], ga_k_norm[j][None, :])
            o = _gqa_flash(q, k, vt, kmax)
            w_o = ga_w_o[j]
        else:
            qkv = _na_qkv(h.reshape(bsz, s, d), na_w_qkv[j].astype(BF16))
            o = _na_attn(qkv, _na_bias_table(na_rpb[j], rows))
            w_o = na_w_o[j]
        h = _proj_ln(o.reshape(m, d), h, w_o.astype(BF16), *ln_p(i, 1))
        h = ffn(h, i, 1)
    return h.reshape(bsz, s, d)
```

```python
import functools
import math

import jax
import jax.numpy as jnp
import numpy as np
from jax import lax
from jax.experimental import pallas as pl
from jax.experimental.pallas import tpu as pltpu

F32 = jnp.float32
BF16 = jnp.bfloat16

GRID_W = 64
HEAD_DIM = 128
N_HEADS = 16
GA_KV_HEADS = 4
GA_GROUPS = N_HEADS // GA_KV_HEADS
ROPE_THETA = 10000.0
NA_WIN_ROWS = 8
NA_WIN_COLS = 16
DEPTH = 2
DEEPNORM_ALPHA = (2 * DEPTH) ** 0.25
LN_EPS = 1e-5
QK_EPS = 1e-6
ATTN_SCALE = HEAD_DIM ** -0.5
LOG2E = math.log2(math.e)
SCORE_SCALE = ATTN_SCALE * LOG2E
NEG_BIG = -1e30
VT_ROWS = HEAD_DIM + 16
BOUND_SLACK = 1.0 + 2.0 ** -5
STABILISER_MAX = 48.0
FLASH_UNROLL = 2

VMEM_LIMIT = 56 * 1024 * 1024
VMEM_LIMIT_FFN = 62 * 1024 * 1024

NA_QROWS = 4
NA_KROWS = NA_QROWS + NA_WIN_ROWS
NA_UNROLL = 4


def _layer_norm(z, g, b):
    mu = jnp.mean(z, axis=-1, keepdims=True)
    zc = z - mu
    var = jnp.mean(zc * zc, axis=-1, keepdims=True)
    return zc * lax.rsqrt(var + LN_EPS) * g + b


def _ffn_ln_kernel(x_ref, wg_ref, wu_ref, wo_ref, g_ref, b_ref, o_ref, xb_ref, *, sub):
    j = pl.program_id(1)

    @pl.when(j == 0)
    def _():
        xb_ref[...] = x_ref[...].astype(BF16)
        o_ref[...] = jnp.zeros(o_ref.shape, F32)

    for r in range(0, xb_ref.shape[0], sub):
        xb = xb_ref[r:r + sub, :]
        gate = jnp.dot(xb, wg_ref[...], preferred_element_type=F32)
        up = jnp.dot(xb, wu_ref[...], preferred_element_type=F32)
        h = (gate * jax.nn.sigmoid(gate) * up).astype(BF16)
        o_ref[r:r + sub, :] += jnp.dot(h, wo_ref[...], preferred_element_type=F32)

    @pl.when(j == pl.num_programs(1) - 1)
    def _():
        z = DEEPNORM_ALPHA * x_ref[...] + 0.5 * o_ref[...]
        o_ref[...] = _layer_norm(z, g_ref[...], b_ref[...])


def _ffn_ln(x2, w_in, w_out, g, b, *, tm=1024, tf=512, sub=512):
    m, d = x2.shape
    f = w_out.shape[0]
    nf = f // tf
    return pl.pallas_call(
        functools.partial(_ffn_ln_kernel, sub=sub),
        out_shape=jax.ShapeDtypeStruct((m, d), F32),
        grid=(m // tm, nf),
        in_specs=[
            pl.BlockSpec((tm, d), lambda i, j: (i, 0)),
            pl.BlockSpec((d, tf), lambda i, j: (0, j)),
            pl.BlockSpec((d, tf), lambda i, j: (0, j + nf)),
            pl.BlockSpec((tf, d), lambda i, j: (j, 0)),
            pl.BlockSpec((1, d), lambda i, j: (0, 0)),
            pl.BlockSpec((1, d), lambda i, j: (0, 0)),
        ],
        out_specs=pl.BlockSpec((tm, d), lambda i, j: (i, 0)),
        scratch_shapes=[pltpu.VMEM((tm, d), BF16)],
        compiler_params=pltpu.CompilerParams(
            dimension_semantics=("parallel", "arbitrary"), vmem_limit_bytes=VMEM_LIMIT_FFN),
        name="ffn_ln",
    )(x2, w_in, w_in, w_out, g, b)


def _proj_ln_kernel(a_ref, x_ref, w_ref, g_ref, b_ref, o_ref, *, sub):
    for r in range(0, a_ref.shape[0], sub):
        y = jnp.dot(a_ref[r:r + sub, :], w_ref[...], preferred_element_type=F32)
        z = DEEPNORM_ALPHA * x_ref[r:r + sub, :] + y
        o_ref[r:r + sub, :] = _layer_norm(z, g_ref[...], b_ref[...])


def _proj_ln(a2, x2, w, g, b, *, tm=1024, sub=256):
    m, d = x2.shape
    k = a2.shape[1]
    return pl.pallas_call(
        functools.partial(_proj_ln_kernel, sub=sub),
        out_shape=jax.ShapeDtypeStruct((m, d), F32),
        grid=(m // tm,),
        in_specs=[
            pl.BlockSpec((tm, k), lambda i: (i, 0)),
            pl.BlockSpec((tm, d), lambda i: (i, 0)),
            pl.BlockSpec((k, d), lambda i: (0, 0)),
            pl.BlockSpec((1, d), lambda i: (0, 0)),
            pl.BlockSpec((1, d), lambda i: (0, 0)),
        ],
        out_specs=pl.BlockSpec((tm, d), lambda i: (i, 0)),
        compiler_params=pltpu.CompilerParams(
            dimension_semantics=("parallel",), vmem_limit_bytes=VMEM_LIMIT),
        name="proj_ln",
    )(a2, x2, w, g, b)


def _score_bound(qg, kg):
    qmax = jnp.max(jnp.abs(qg), axis=1, keepdims=True)
    kmax = jnp.max(jnp.abs(kg), axis=1, keepdims=True)
    return (SCORE_SCALE * HEAD_DIM * BOUND_SLACK) * qmax * kmax


def _gqa_qkv_kernel(x_ref, w_ref, cos_ref, sin_ref, qg_ref, kg_ref, q_ref, k_ref, vt_ref,
                    *, heads_per_dot):
    tm = x_ref.shape[1]
    xb = x_ref[0].astype(BF16)
    lane0 = lax.broadcasted_iota(jnp.int32, (tm, HEAD_DIM), 1) == 0
    cos = cos_ref[...]
    sin = sin_ref[...]
    q_cos = cos * (qg_ref[0:1, :] * SCORE_SCALE)
    q_sin = sin * (qg_ref[1:2, :] * SCORE_SCALE)
    k_cos = cos * kg_ref[0:1, :]
    k_sin = sin * kg_ref[1:2, :]

    def norm_rope(yh, c, s):
        ms = jnp.mean(yh * yh, axis=-1, keepdims=True)
        yn = yh * lax.rsqrt(ms + QK_EPS)
        return yn * c + pltpu.roll(yn, HEAD_DIM // 2, 1) * s

    q_hi = jnp.where(lane0, -_score_bound(qg_ref[0:1, :], kg_ref[0:1, :]), 0.0).astype(BF16)
    k_hi = jnp.where(lane0, 1.0, 0.0).astype(BF16)
    ones = jnp.ones((VT_ROWS - HEAD_DIM, tm), BF16)

    n_all = N_HEADS + 2 * GA_KV_HEADS

    def project(h0):
        c0 = h0 * HEAD_DIM
        return jnp.dot(xb, w_ref[:, c0:c0 + heads_per_dot * HEAD_DIM], preferred_element_type=F32)

    y_next = project(0)
    for h0 in range(0, n_all, heads_per_dot):
        y = y_next
        if h0 + heads_per_dot < n_all:
            y_next = project(h0 + heads_per_dot)
        for t in range(heads_per_dot):
            h = h0 + t
            yh = y[:, t * HEAD_DIM:(t + 1) * HEAD_DIM]
            if h < N_HEADS:
                q_ref[0, h, :, :HEAD_DIM] = norm_rope(yh, q_cos, q_sin).astype(BF16)
                q_ref[0, h, :, HEAD_DIM:] = q_hi
            elif h < N_HEADS + GA_KV_HEADS:
                k_ref[0, h - N_HEADS, :, :HEAD_DIM] = norm_rope(yh, k_cos, k_sin).astype(BF16)
                k_ref[0, h - N_HEADS, :, HEAD_DIM:] = k_hi
            else:
                hv = h - N_HEADS - GA_KV_HEADS
                vt_ref[0, hv, 0, :HEAD_DIM, :] = yh.T.astype(BF16)
                vt_ref[0, hv, 0, HEAD_DIM:, :] = ones


def _gqa_qkv(x3, w, cos, sin, qg, kg, *, tm=512, heads_per_dot=4):
    bsz, s, d = x3.shape
    n = w.shape[1]
    hd = HEAD_DIM
    return pl.pallas_call(
        functools.partial(_gqa_qkv_kernel, heads_per_dot=heads_per_dot),
        out_shape=(
            jax.ShapeDtypeStruct((bsz, N_HEADS, s, 2 * hd), BF16),
            jax.ShapeDtypeStruct((bsz, GA_KV_HEADS, s, 2 * hd), BF16),
            jax.ShapeDtypeStruct((bsz, GA_KV_HEADS, s // tm, VT_ROWS, tm), BF16),
        ),
        grid=(bsz, s // tm),
        in_specs=[
            pl.BlockSpec((1, tm, d), lambda b, i: (b, i, 0)),
            pl.BlockSpec((d, n), lambda b, i: (0, 0)),
            pl.BlockSpec((tm, hd), lambda b, i: (i, 0)),
            pl.BlockSpec((tm, hd), lambda b, i: (i, 0)),
            pl.BlockSpec((2, hd), lambda b, i: (0, 0)),
            pl.BlockSpec((2, hd), lambda b, i: (0, 0)),
        ],
        out_specs=(
            pl.BlockSpec((1, N_HEADS, tm, 2 * hd), lambda b, i: (b, 0, i, 0)),
            pl.BlockSpec((1, GA_KV_HEADS, tm, 2 * hd), lambda b, i: (b, 0, i, 0)),
            pl.BlockSpec((1, GA_KV_HEADS, 1, VT_ROWS, tm), lambda b, i: (b, 0, i, 0, 0)),
        ),
        compiler_params=pltpu.CompilerParams(
            dimension_semantics=("parallel", "parallel"), vmem_limit_bytes=VMEM_LIMIT),
        name="gqa_qkv",
    )(x3, w, cos, sin, qg, kg)


def _gqa_flash_kernel(q_ref, k_ref, vt_ref, qg_ref, kg_ref, o_ref, q2_ref, mx_ref, acc_ref, st_ref,
                      *, tq, tk):
    hd = HEAD_DIM
    nk = k_ref.shape[2] // tk
    q2_ref[...] = q_ref[0]

    @pl.when(jnp.max(_score_bound(qg_ref[0:1, :], kg_ref[0:1, :])) > STABILISER_MAX)
    def _():
        lane0 = lax.broadcasted_iota(jnp.int32, (tq, hd), 1) == 0
        for h in range(GA_GROUPS):
            mx_ref[...] = jnp.full(mx_ref.shape, NEG_BIG, F32)

            def max_body(i, carry):
                off = pl.multiple_of(i * tk, tk)
                k = k_ref[0, 0, pl.ds(off, tk), :hd]
                s = lax.dot_general(q_ref[0, h, :, :hd], k, (((1,), (1,)), ((), ())),
                                    preferred_element_type=F32)
                mx_ref[...] = jnp.maximum(mx_ref[...], jnp.max(s, axis=1, keepdims=True))
                return carry

            lax.fori_loop(0, nk, max_body, 0)
            q2_ref[h, :, hd:] = jnp.where(lane0, -mx_ref[...], 0.0).astype(BF16)

    acc_ref[...] = jnp.zeros(acc_ref.shape, F32)

    def scores(i, h):
        off = pl.multiple_of(i * tk, tk)
        return lax.dot_general(k_ref[0, 0, pl.ds(off, tk), :], q2_ref[h], (((1,), (1,)), ((), ())),
                               preferred_element_type=F32)

    st_ref[...] = scores(0, 0)

    def body(it, carry):
        st = st_ref[...]
        for u in range(FLASH_UNROLL):
            i = it * FLASH_UNROLL + u
            vt = vt_ref[0, 0, i]
            for h in range(GA_GROUPS):
                if h + 1 < GA_GROUPS:
                    st_next = scores(i, h + 1)
                else:
                    st_next = scores(jnp.minimum(i + 1, nk - 1), 0)
                p = jnp.exp2(st).astype(BF16)
                acc_ref[h] += jnp.dot(vt, p, preferred_element_type=F32)
                st = st_next
        st_ref[...] = st
        return carry

    lax.fori_loop(0, nk // FLASH_UNROLL, body, 0)
    for h in range(GA_GROUPS):
        out_t = acc_ref[h, :hd, :] / acc_ref[h, hd:hd + 1, :]
        o_ref[0, :, h * hd:(h + 1) * hd] = out_t.T.astype(BF16)


def _gqa_flash(q, k, vt, qg, kg, *, tq=512):
    bsz, _, s, hd2 = q.shape
    hd = hd2 // 2
    nk, tk = vt.shape[2], vt.shape[4]
    return pl.pallas_call(
        functools.partial(_gqa_flash_kernel, tq=tq, tk=tk),
        out_shape=jax.ShapeDtypeStruct((bsz, s, N_HEADS * hd), BF16),
        grid=(bsz, GA_KV_HEADS, s // tq),
        in_specs=[
            pl.BlockSpec((1, GA_GROUPS, tq, hd2), lambda b, g, i: (b, g, i, 0)),
            pl.BlockSpec((1, 1, s, hd2), lambda b, g, i: (b, g, 0, 0)),
            pl.BlockSpec((1, 1, nk, VT_ROWS, tk), lambda b, g, i: (b, g, 0, 0, 0)),
            pl.BlockSpec((2, hd), lambda b, g, i: (0, 0)),
            pl.BlockSpec((2, hd), lambda b, g, i: (0, 0)),
        ],
        out_specs=pl.BlockSpec((1, tq, GA_GROUPS * hd), lambda b, g, i: (b, i, g)),
        scratch_shapes=[pltpu.VMEM((GA_GROUPS, tq, hd2), BF16), pltpu.VMEM((tq, hd), F32),
                        pltpu.VMEM((GA_GROUPS, VT_ROWS, tq), F32), pltpu.VMEM((tk, tq), F32)],
        compiler_params=pltpu.CompilerParams(
            dimension_semantics=("parallel", "parallel", "parallel"), vmem_limit_bytes=VMEM_LIMIT),
        name="gqa_flash",
    )(q, k, vt, qg, kg)


def _na_qkv_kernel(x_ref, w_ref, o_ref, *, q_tiles):
    xb = x_ref[0].astype(BF16)
    y = jnp.dot(xb, w_ref[...], preferred_element_type=F32)
    y = y * jnp.where(pl.program_id(2) < q_tiles, SCORE_SCALE, 1.0)
    for t in range(o_ref.shape[1]):
        o_ref[0, t] = y[:, t * HEAD_DIM:(t + 1) * HEAD_DIM].astype(BF16)


def _na_qkv(x3, w, *, tm=1024, tn=1024):
    bsz, s, d = x3.shape
    n = w.shape[1]
    hd = HEAD_DIM
    return pl.pallas_call(
        functools.partial(_na_qkv_kernel, q_tiles=N_HEADS * hd // tn),
        out_shape=jax.ShapeDtypeStruct((bsz, n // hd, s, hd), BF16),
        grid=(bsz, s // tm, n // tn),
        in_specs=[
            pl.BlockSpec((1, tm, d), lambda b, i, j: (b, i, 0)),
            pl.BlockSpec((d, tn), lambda b, i, j: (0, j)),
        ],
        out_specs=pl.BlockSpec((1, tn // hd, tm, hd), lambda b, i, j: (b, j, i, 0)),
        compiler_params=pltpu.CompilerParams(
            dimension_semantics=("parallel", "parallel", "arbitrary"), vmem_limit_bytes=VMEM_LIMIT),
        name="na_qkv",
    )(x3, w)


def _na_attn_kernel(q_ref, k_ref, v_ref, bias_ref, o_ref, va_ref, *, rows):
    hd = HEAD_DIM
    nq = NA_QROWS * GRID_W
    nkeys = NA_KROWS * GRID_W
    nblk = rows // NA_QROWS
    va_ref[:, :hd] = v_ref[0, 0]
    va_ref[:, hd:] = jnp.ones((va_ref.shape[0], hd), BF16)

    def block_scores(rb):
        qoff = pl.multiple_of(rb * nq, nq)
        start = jnp.clip(rb * NA_QROWS - NA_WIN_ROWS // 2, 0, rows - NA_KROWS)
        koff = pl.multiple_of(start * GRID_W, GRID_W)
        cls = jnp.where(rb == 0, 0, jnp.where(rb == nblk - 1, 2, 1))
        q = q_ref[0, 0, pl.ds(qoff, nq), :]
        k = k_ref[0, 0, pl.ds(koff, nkeys), :]
        s = lax.dot_general(q, k, (((1,), (1,)), ((), ())), preferred_element_type=F32)
        return qoff, koff, s + bias_ref[cls, 0]

    def body(it, carry):
        items = [block_scores(it * NA_UNROLL + u) for u in range(NA_UNROLL)]
        for qoff, koff, s in items:
            m = jnp.max(s, axis=1, keepdims=True)
            p = jnp.exp2(s - m).astype(BF16)
            oa = jnp.dot(p, va_ref[pl.ds(koff, nkeys), :], preferred_element_type=F32)
            o_ref[0, pl.ds(qoff, nq), :] = (oa[:, :hd] / oa[:, hd:hd + 1]).astype(BF16)
        return carry

    lax.fori_loop(0, nblk // NA_UNROLL, body, 0)


def _na_attn(qkv, bias):
    bsz, _, s, hd = qkv.shape
    rows = s // GRID_W
    nq = NA_QROWS * GRID_W
    nkeys = NA_KROWS * GRID_W
    return pl.pallas_call(
        functools.partial(_na_attn_kernel, rows=rows),
        out_shape=jax.ShapeDtypeStruct((bsz, s, N_HEADS * hd), BF16),
        grid=(bsz, N_HEADS),
        in_specs=[
            pl.BlockSpec((1, 1, s, hd), lambda b, h: (b, h, 0, 0)),
            pl.BlockSpec((1, 1, s, hd), lambda b, h: (b, N_HEADS + h, 0, 0)),
            pl.BlockSpec((1, 1, s, hd), lambda b, h: (b, 2 * N_HEADS + h, 0, 0)),
            pl.BlockSpec((3, 1, nq, nkeys), lambda b, h: (0, h, 0, 0)),
        ],
        out_specs=pl.BlockSpec((1, s, hd), lambda b, h: (b, 0, h)),
        scratch_shapes=[pltpu.VMEM((s, 2 * hd), BF16)],
        compiler_params=pltpu.CompilerParams(
            dimension_semantics=("parallel", "parallel"), vmem_limit_bytes=VMEM_LIMIT),
        name="na_attn",
    )(qkv, qkv, qkv, bias)


def _na_bias_table(rpb, rows):
    h = rpb.shape[0]
    c = np.arange(GRID_W)
    col_start = np.clip(c - NA_WIN_COLS // 2, 0, GRID_W - NA_WIN_COLS)
    kc = np.arange(GRID_W)
    col_rel = kc[None, :] - c[:, None] + (NA_WIN_COLS - 1)
    col_ok = (kc[None, :] >= col_start[:, None]) & (kc[None, :] < col_start[:, None] + NA_WIN_COLS)
    col_rel = np.clip(col_rel, 0, 2 * NA_WIN_COLS - 2)
    t = jnp.where(jnp.asarray(col_ok)[None, None], rpb[:, :, jnp.asarray(col_rel)] * LOG2E, NEG_BIG)
    neg_idx = 2 * NA_WIN_ROWS - 1
    t = jnp.concatenate([t, jnp.full((h, 1, GRID_W, GRID_W), NEG_BIG, F32)], axis=1)

    nblk = rows // NA_QROWS
    idx = np.full((3, NA_QROWS, NA_KROWS), neg_idx, np.int32)
    for cls, rb in enumerate((0, 1, nblk - 1)):
        r0 = rb * NA_QROWS
        start = int(np.clip(r0 - NA_WIN_ROWS // 2, 0, rows - NA_KROWS))
        for ri in range(NA_QROWS):
            r = r0 + ri
            row_start = int(np.clip(r - NA_WIN_ROWS // 2, 0, rows - NA_WIN_ROWS))
            for ki in range(NA_KROWS):
                kr = start + ki
                if row_start <= kr < row_start + NA_WIN_ROWS:
                    idx[cls, ri, ki] = kr - r + (NA_WIN_ROWS - 1)
    blocks = jnp.take(t, jnp.asarray(idx.reshape(-1)), axis=1)
    blocks = blocks.reshape(h, 3, NA_QROWS, NA_KROWS, GRID_W, GRID_W)
    blocks = blocks.transpose(1, 0, 2, 4, 3, 5)
    return blocks.reshape(3, h, NA_QROWS * GRID_W, NA_KROWS * GRID_W)


def _rope_tables(seq):
    t = jnp.arange(seq)
    row = (t // GRID_W).astype(F32)
    col = (t % GRID_W).astype(F32)
    axis_dim = HEAD_DIM // 2
    inv_freq = ROPE_THETA ** (-jnp.arange(0, axis_dim, 2, dtype=F32) / axis_dim)
    ar = row[:, None] * inv_freq
    ac = col[:, None] * inv_freq
    cos = jnp.concatenate([jnp.cos(ar), jnp.cos(ac), jnp.cos(ar), jnp.cos(ac)], axis=1)
    sin = jnp.concatenate([-jnp.sin(ar), -jnp.sin(ac), jnp.sin(ar), jnp.sin(ac)], axis=1)
    return cos, sin


def _rope_layout(a, n_heads):
    nf = HEAD_DIM // 4
    lead = a.shape[:-1]
    n = n_heads * HEAD_DIM
    p = a[..., :n].reshape(*lead, n_heads, 2, 2, nf)
    p = jnp.swapaxes(p, -3, -2).reshape(*lead, n)
    return jnp.concatenate([p, a[..., n:]], axis=-1)


def _gain_rows(g):
    g = _rope_layout(g, 1)
    return jnp.stack([g, jnp.roll(g, HEAD_DIM // 2)])


def kernel(x, ln_g, ln_b, ffn_w_in, ffn_w_out, ga_w_qkv, ga_q_norm, ga_k_norm, ga_w_o,
           na_w_qkv, na_rpb, na_w_o):
    bsz, s, d = x.shape
    m = bsz * s
    rows = s // GRID_W
    cos, sin = _rope_tables(s)

    def ln_p(i, k):
        return ln_g[i, k][None, :], ln_b[i, k][None, :]

    def ffn(h, i, k):
        return _ffn_ln(h, ffn_w_in[i, k].astype(BF16), ffn_w_out[i, k].astype(BF16), *ln_p(i, 2 * k))

    h = x.reshape(m, d)
    for i in range(DEPTH):
        h = ffn(h, i, 0)
        j = i // 2
        if i % 2 == 0:
            qg, kg = _gain_rows(ga_q_norm[j]), _gain_rows(ga_k_norm[j])
            w_qkv = _rope_layout(ga_w_qkv[j], N_HEADS + GA_KV_HEADS).astype(BF16)
            q, k, vt = _gqa_qkv(h.reshape(bsz, s, d), w_qkv, cos, sin, qg, kg)
            o = _gqa_flash(q, k, vt, qg, kg)
            w_o = ga_w_o[j]
        else:
            qkv = _na_qkv(h.reshape(bsz, s, d), na_w_qkv[j].astype(BF16))
            o = _na_attn(qkv, _na_bias_table(na_rpb[j], rows))
            w_o = na_w_o[j]
        h = _proj_ln(o.reshape(m, d), h, w_o.astype(BF16), *ln_p(i, 1))
        h = ffn(h, i, 1)
    return h.reshape(bsz, s, d)
```

```python
import functools
import math

import jax
import jax.numpy as jnp
import numpy as np
from jax import lax
from jax.experimental import pallas as pl
from jax.experimental.pallas import tpu as pltpu

F32 = jnp.float32
BF16 = jnp.bfloat16

GRID_W = 64
HEAD_DIM = 128
N_HEADS = 16
GA_KV_HEADS = 4
GA_GROUPS = N_HEADS // GA_KV_HEADS
ROPE_THETA = 10000.0
NA_WIN_ROWS = 8
NA_WIN_COLS = 16
DEPTH = 2
DEEPNORM_ALPHA = (2 * DEPTH) ** 0.25
LN_EPS = 1e-5
QK_EPS = 1e-6
ATTN_SCALE = HEAD_DIM ** -0.5
LOG2E = math.log2(math.e)
SCORE_SCALE = ATTN_SCALE * LOG2E
NEG_BIG = -1e30
VT_ROWS = HEAD_DIM + 16
BOUND_SLACK = 1.0 + 2.0 ** -5
STABILISER_MAX = 48.0
FLASH_UNROLL = 4

VMEM_LIMIT = 56 * 1024 * 1024
VMEM_LIMIT_FFN = 62 * 1024 * 1024

NA_QROWS = 4
NA_KROWS = NA_QROWS + NA_WIN_ROWS
NA_UNROLL = 4


def _layer_norm(z, g, b):
    mu = jnp.mean(z, axis=-1, keepdims=True)
    zc = z - mu
    var = jnp.mean(zc * zc, axis=-1, keepdims=True)
    return zc * lax.rsqrt(var + LN_EPS) * g + b


def _ffn_ln_kernel(x_ref, wg_ref, wu_ref, wo_ref, g_ref, b_ref, o_ref, xb_ref, *, sub):
    j = pl.program_id(1)

    @pl.when(j == 0)
    def _():
        xb_ref[...] = x_ref[...].astype(BF16)
        o_ref[...] = jnp.zeros(o_ref.shape, F32)

    for r in range(0, xb_ref.shape[0], sub):
        xb = xb_ref[r:r + sub, :]
        gate = jnp.dot(xb, wg_ref[...], preferred_element_type=F32)
        up = jnp.dot(xb, wu_ref[...], preferred_element_type=F32)
        h = (gate * jax.nn.sigmoid(gate) * up).astype(BF16)
        o_ref[r:r + sub, :] += jnp.dot(h, wo_ref[...], preferred_element_type=F32)

    @pl.when(j == pl.num_programs(1) - 1)
    def _():
        z = DEEPNORM_ALPHA * x_ref[...] + 0.5 * o_ref[...]
        o_ref[...] = _layer_norm(z, g_ref[...], b_ref[...])


def _ffn_ln(x2, w_in, w_out, g, b, layer, which, *, tm=1024, tf=512, sub=512):
    m, d = x2.shape
    f = w_out.shape[2]
    nf = f // tf
    assert m % tm == 0 and tm % sub == 0 and f % tf == 0
    return pl.pallas_call(
        functools.partial(_ffn_ln_kernel, sub=sub),
        out_shape=jax.ShapeDtypeStruct((m, d), F32),
        grid=(m // tm, nf),
        in_specs=[
            pl.BlockSpec((tm, d), lambda i, j: (i, 0)),
            pl.BlockSpec((None, None, d, tf), lambda i, j: (layer, which, 0, j)),
            pl.BlockSpec((None, None, d, tf), lambda i, j: (layer, which, 0, j + nf)),
            pl.BlockSpec((None, None, tf, d), lambda i, j: (layer, which, j, 0)),
            pl.BlockSpec((1, d), lambda i, j: (0, 0)),
            pl.BlockSpec((1, d), lambda i, j: (0, 0)),
        ],
        out_specs=pl.BlockSpec((tm, d), lambda i, j: (i, 0)),
        scratch_shapes=[pltpu.VMEM((tm, d), BF16)],
        compiler_params=pltpu.CompilerParams(
            dimension_semantics=("parallel", "arbitrary"), vmem_limit_bytes=VMEM_LIMIT_FFN),
        name="ffn_ln",
    )(x2, w_in, w_in, w_out, g, b)


def _proj_ln_kernel(a_ref, x_ref, w_ref, g_ref, b_ref, o_ref, *, sub):
    for r in range(0, a_ref.shape[0], sub):
        y = jnp.dot(a_ref[r:r + sub, :], w_ref[...], preferred_element_type=F32)
        z = DEEPNORM_ALPHA * x_ref[r:r + sub, :] + y
        o_ref[r:r + sub, :] = _layer_norm(z, g_ref[...], b_ref[...])


def _proj_ln(a2, x2, w, g, b, *, tm=1024, sub=256):
    m, d = x2.shape
    k = a2.shape[1]
    return pl.pallas_call(
        functools.partial(_proj_ln_kernel, sub=sub),
        out_shape=jax.ShapeDtypeStruct((m, d), F32),
        grid=(m // tm,),
        in_specs=[
            pl.BlockSpec((tm, k), lambda i: (i, 0)),
            pl.BlockSpec((tm, d), lambda i: (i, 0)),
            pl.BlockSpec((k, d), lambda i: (0, 0)),
            pl.BlockSpec((1, d), lambda i: (0, 0)),
            pl.BlockSpec((1, d), lambda i: (0, 0)),
        ],
        out_specs=pl.BlockSpec((tm, d), lambda i: (i, 0)),
        compiler_params=pltpu.CompilerParams(
            dimension_semantics=("parallel",), vmem_limit_bytes=VMEM_LIMIT),
        name="proj_ln",
    )(a2, x2, w, g, b)


def _score_bound(qg, kg):
    qmax = jnp.max(jnp.abs(qg), axis=1, keepdims=True)
    kmax = jnp.max(jnp.abs(kg), axis=1, keepdims=True)
    return (SCORE_SCALE * HEAD_DIM * BOUND_SLACK) * qmax * kmax


def _gqa_qkv_kernel(x_ref, w_ref, cos_ref, sin_ref, qg_ref, kg_ref, q_ref, k_ref, vt_ref,
                    *, heads_per_dot):
    tm = x_ref.shape[1]
    xb = x_ref[0].astype(BF16)
    lane0 = lax.broadcasted_iota(jnp.int32, (tm, HEAD_DIM), 1) == 0
    cos = cos_ref[...]
    sin = sin_ref[...]
    q_cos = cos * (qg_ref[0:1, :] * SCORE_SCALE)
    q_sin = sin * (qg_ref[1:2, :] * SCORE_SCALE)
    k_cos = cos * kg_ref[0:1, :]
    k_sin = sin * kg_ref[1:2, :]

    def norm_rope(yh, c, s):
        ms = jnp.mean(yh * yh, axis=-1, keepdims=True)
        yn = yh * lax.rsqrt(ms + QK_EPS)
        return yn * c + pltpu.roll(yn, HEAD_DIM // 2, 1) * s

    q_hi = jnp.where(lane0, -_score_bound(qg_ref[0:1, :], kg_ref[0:1, :]), 0.0).astype(BF16)
    k_hi = jnp.where(lane0, 1.0, 0.0).astype(BF16)
    ones = jnp.ones((VT_ROWS - HEAD_DIM, tm), BF16)

    n_all = N_HEADS + 2 * GA_KV_HEADS

    def project(h0):
        c0 = h0 * HEAD_DIM
        return jnp.dot(xb, w_ref[:, c0:c0 + heads_per_dot * HEAD_DIM], preferred_element_type=F32)

    y_next = project(0)
    for h0 in range(0, n_all, heads_per_dot):
        y = y_next
        if h0 + heads_per_dot < n_all:
            y_next = project(h0 + heads_per_dot)
        for t in range(heads_per_dot):
            h = h0 + t
            yh = y[:, t * HEAD_DIM:(t + 1) * HEAD_DIM]
            if h < N_HEADS:
                q_ref[0, h, :, :HEAD_DIM] = norm_rope(yh, q_cos, q_sin).astype(BF16)
                q_ref[0, h, :, HEAD_DIM:] = q_hi
            elif h < N_HEADS + GA_KV_HEADS:
                k_ref[0, h - N_HEADS, :, :HEAD_DIM] = norm_rope(yh, k_cos, k_sin).astype(BF16)
                k_ref[0, h - N_HEADS, :, HEAD_DIM:] = k_hi
            else:
                hv = h - N_HEADS - GA_KV_HEADS
                vt_ref[0, hv, 0, :HEAD_DIM, :] = yh.T.astype(BF16)
                vt_ref[0, hv, 0, HEAD_DIM:, :] = ones


def _gqa_qkv(x3, w, cos, sin, qg, kg, *, tm=512, heads_per_dot=4):
    bsz, s, d = x3.shape
    n = w.shape[1]
    hd = HEAD_DIM
    return pl.pallas_call(
        functools.partial(_gqa_qkv_kernel, heads_per_dot=heads_per_dot),
        out_shape=(
            jax.ShapeDtypeStruct((bsz, N_HEADS, s, 2 * hd), BF16),
            jax.ShapeDtypeStruct((bsz, GA_KV_HEADS, s, 2 * hd), BF16),
            jax.ShapeDtypeStruct((bsz, GA_KV_HEADS, s // tm, VT_ROWS, tm), BF16),
        ),
        grid=(bsz, s // tm),
        in_specs=[
            pl.BlockSpec((1, tm, d), lambda b, i: (b, i, 0)),
            pl.BlockSpec((d, n), lambda b, i: (0, 0)),
            pl.BlockSpec((tm, hd), lambda b, i: (i, 0)),
            pl.BlockSpec((tm, hd), lambda b, i: (i, 0)),
            pl.BlockSpec((2, hd), lambda b, i: (0, 0)),
            pl.BlockSpec((2, hd), lambda b, i: (0, 0)),
        ],
        out_specs=(
            pl.BlockSpec((1, N_HEADS, tm, 2 * hd), lambda b, i: (b, 0, i, 0)),
            pl.BlockSpec((1, GA_KV_HEADS, tm, 2 * hd), lambda b, i: (b, 0, i, 0)),
            pl.BlockSpec((1, GA_KV_HEADS, 1, VT_ROWS, tm), lambda b, i: (b, 0, i, 0, 0)),
        ),
        compiler_params=pltpu.CompilerParams(
            dimension_semantics=("parallel", "parallel"), vmem_limit_bytes=VMEM_LIMIT),
        name="gqa_qkv",
    )(x3, w, cos, sin, qg, kg)


def _gqa_flash_kernel(q_ref, k_ref, vt_ref, qg_ref, kg_ref, o_ref, qt_ref, mx_ref, acc_ref, st_ref,
                      *, tq, tk):
    hd = HEAD_DIM
    nk = k_ref.shape[2] // tk
    for h in range(GA_GROUPS):
        qt_ref[h] = q_ref[0, h].T

    @pl.when(jnp.max(_score_bound(qg_ref[0:1, :], kg_ref[0:1, :])) > STABILISER_MAX)
    def _():
        row0 = lax.broadcasted_iota(jnp.int32, (16, tq), 0) == 0
        for h in range(GA_GROUPS):
            mx_ref[...] = jnp.full(mx_ref.shape, NEG_BIG, F32)

            def max_body(i, carry):
                off = pl.multiple_of(i * tk, tk)
                s = jnp.dot(k_ref[0, 0, pl.ds(off, tk), :hd], qt_ref[h, :hd, :],
                            preferred_element_type=F32)
                mx_ref[...] = jnp.maximum(mx_ref[...], jnp.max(s, axis=0, keepdims=True))
                return carry

            lax.fori_loop(0, nk, max_body, 0)
            qt_ref[h, hd:hd + 16, :] = jnp.where(row0, -mx_ref[...], 0.0).astype(BF16)

    acc_ref[...] = jnp.zeros(acc_ref.shape, F32)

    def scores(i, h):
        off = pl.multiple_of(i * tk, tk)
        return jnp.dot(k_ref[0, 0, pl.ds(off, tk), :], qt_ref[h],
                       preferred_element_type=F32)

    st_ref[...] = scores(0, 0)

    def body(it, carry):
        st = st_ref[...]
        for u in range(FLASH_UNROLL):
            i = it * FLASH_UNROLL + u
            vt = vt_ref[0, 0, i]
            for h in range(GA_GROUPS):
                if h + 1 < GA_GROUPS:
                    st_next = scores(i, h + 1)
                else:
                    st_next = scores(jnp.minimum(i + 1, nk - 1), 0)
                p = jnp.exp2(st).astype(BF16)
                acc_ref[h] += jnp.dot(vt, p, preferred_element_type=F32)
                st = st_next
        st_ref[...] = st
        return carry

    lax.fori_loop(0, nk // FLASH_UNROLL, body, 0)
    for h in range(GA_GROUPS):
        out_t = acc_ref[h, :hd, :] / acc_ref[h, hd:hd + 1, :]
        o_ref[0, :, h * hd:(h + 1) * hd] = out_t.T.astype(BF16)


def _gqa_flash(q, k, vt, qg, kg, *, tq=512):
    bsz, _, s, hd2 = q.shape
    hd = hd2 // 2
    nk, tk = vt.shape[2], vt.shape[4]
    assert s % tq == 0 and nk % FLASH_UNROLL == 0
    return pl.pallas_call(
        functools.partial(_gqa_flash_kernel, tq=tq, tk=tk),
        out_shape=jax.ShapeDtypeStruct((bsz, s, N_HEADS * hd), BF16),
        grid=(bsz, GA_KV_HEADS, s // tq),
        in_specs=[
            pl.BlockSpec((1, GA_GROUPS, tq, hd2), lambda b, g, i: (b, g, i, 0)),
            pl.BlockSpec((1, 1, s, hd2), lambda b, g, i: (b, g, 0, 0)),
            pl.BlockSpec((1, 1, nk, VT_ROWS, tk), lambda b, g, i: (b, g, 0, 0, 0)),
            pl.BlockSpec((2, hd), lambda b, g, i: (0, 0)),
            pl.BlockSpec((2, hd), lambda b, g, i: (0, 0)),
        ],
        out_specs=pl.BlockSpec((1, tq, GA_GROUPS * hd), lambda b, g, i: (b, i, g)),
        scratch_shapes=[pltpu.VMEM((GA_GROUPS, hd2, tq), BF16), pltpu.VMEM((1, tq), F32),
                        pltpu.VMEM((GA_GROUPS, VT_ROWS, tq), F32), pltpu.VMEM((tk, tq), F32)],
        compiler_params=pltpu.CompilerParams(
            dimension_semantics=("parallel", "parallel", "parallel"), vmem_limit_bytes=VMEM_LIMIT),
        name="gqa_flash",
    )(q, k, vt, qg, kg)


def _na_qkv_kernel(x_ref, w_ref, o_ref, *, q_tiles):
    xb = x_ref[0].astype(BF16)
    y = jnp.dot(xb, w_ref[...], preferred_element_type=F32)
    y = y * jnp.where(pl.program_id(2) < q_tiles, SCORE_SCALE, 1.0)
    for t in range(o_ref.shape[1]):
        o_ref[0, t] = y[:, t * HEAD_DIM:(t + 1) * HEAD_DIM].astype(BF16)


def _na_qkv(x3, w, *, tm=1024, tn=1024):
    bsz, s, d = x3.shape
    n = w.shape[1]
    hd = HEAD_DIM
    return pl.pallas_call(
        functools.partial(_na_qkv_kernel, q_tiles=N_HEADS * hd // tn),
        out_shape=jax.ShapeDtypeStruct((bsz, n // hd, s, hd), BF16),
        grid=(bsz, s // tm, n // tn),
        in_specs=[
            pl.BlockSpec((1, tm, d), lambda b, i, j: (b, i, 0)),
            pl.BlockSpec((d, tn), lambda b, i, j: (0, j)),
        ],
        out_specs=pl.BlockSpec((1, tn // hd, tm, hd), lambda b, i, j: (b, j, i, 0)),
        compiler_params=pltpu.CompilerParams(
            dimension_semantics=("parallel", "parallel", "arbitrary"), vmem_limit_bytes=VMEM_LIMIT),
        name="na_qkv",
    )(x3, w)


def _na_attn_kernel(q_ref, k_ref, v_ref, bias_ref, o_ref, va_ref, *, rows):
    hd = HEAD_DIM
    nq = NA_QROWS * GRID_W
    nkeys = NA_KROWS * GRID_W
    nblk = rows // NA_QROWS
    va_ref[:, :hd] = v_ref[0, 0]
    va_ref[:, hd:] = jnp.ones((va_ref.shape[0], hd), BF16)

    def block_scores(rb):
        qoff = pl.multiple_of(rb * nq, nq)
        start = jnp.clip(rb * NA_QROWS - NA_WIN_ROWS // 2, 0, rows - NA_KROWS)
        koff = pl.multiple_of(start * GRID_W, GRID_W)
        cls = jnp.where(rb == 0, 0, jnp.where(rb == nblk - 1, 2, 1))
        q = q_ref[0, 0, pl.ds(qoff, nq), :]
        k = k_ref[0, 0, pl.ds(koff, nkeys), :]
        s = lax.dot_general(q, k, (((1,), (1,)), ((), ())), preferred_element_type=F32)
        return qoff, koff, s + bias_ref[cls, 0]

    def body(it, carry):
        items = [block_scores(it * NA_UNROLL + u) for u in range(NA_UNROLL)]
        for qoff, koff, s in items:
            m = jnp.max(s, axis=1, keepdims=True)
            p = jnp.exp2(s - m).astype(BF16)
            oa = jnp.dot(p, va_ref[pl.ds(koff, nkeys), :], preferred_element_type=F32)
            o_ref[0, pl.ds(qoff, nq), :] = (oa[:, :hd] / oa[:, hd:hd + 1]).astype(BF16)
        return carry

    lax.fori_loop(0, nblk // NA_UNROLL, body, 0)


def _na_attn(qkv, bias):
    bsz, _, s, hd = qkv.shape
    rows = s // GRID_W
    nq = NA_QROWS * GRID_W
    nkeys = NA_KROWS * GRID_W
    assert rows >= NA_KROWS and rows % (NA_QROWS * NA_UNROLL) == 0
    return pl.pallas_call(
        functools.partial(_na_attn_kernel, rows=rows),
        out_shape=jax.ShapeDtypeStruct((bsz, s, N_HEADS * hd), BF16),
        grid=(bsz, N_HEADS),
        in_specs=[
            pl.BlockSpec((1, 1, s, hd), lambda b, h: (b, h, 0, 0)),
            pl.BlockSpec((1, 1, s, hd), lambda b, h: (b, N_HEADS + h, 0, 0)),
            pl.BlockSpec((1, 1, s, hd), lambda b, h: (b, 2 * N_HEADS + h, 0, 0)),
            pl.BlockSpec((3, 1, nq, nkeys), lambda b, h: (0, h, 0, 0)),
        ],
        out_specs=pl.BlockSpec((1, s, hd), lambda b, h: (b, 0, h)),
        scratch_shapes=[pltpu.VMEM((s, 2 * hd), BF16)],
        compiler_params=pltpu.CompilerParams(
            dimension_semantics=("parallel", "parallel"), vmem_limit_bytes=VMEM_LIMIT),
        name="na_attn",
    )(qkv, qkv, qkv, bias)


def _na_bias_table(rpb, rows):
    h = rpb.shape[0]
    c = np.arange(GRID_W)
    col_start = np.clip(c - NA_WIN_COLS // 2, 0, GRID_W - NA_WIN_COLS)
    kc = np.arange(GRID_W)
    col_rel = kc[None, :] - c[:, None] + (NA_WIN_COLS - 1)
    col_ok = (kc[None, :] >= col_start[:, None]) & (kc[None, :] < col_start[:, None] + NA_WIN_COLS)
    col_rel = np.clip(col_rel, 0, 2 * NA_WIN_COLS - 2)
    t = jnp.where(jnp.asarray(col_ok)[None, None], rpb[:, :, jnp.asarray(col_rel)] * LOG2E, NEG_BIG)
    neg_idx = 2 * NA_WIN_ROWS - 1
    t = jnp.concatenate([t, jnp.full((h, 1, GRID_W, GRID_W), NEG_BIG, F32)], axis=1)

    nblk = rows // NA_QROWS
    idx = np.full((3, NA_QROWS, NA_KROWS), neg_idx, np.int32)
    for cls, rb in enumerate((0, 1, nblk - 1)):
        r0 = rb * NA_QROWS
        start = int(np.clip(r0 - NA_WIN_ROWS // 2, 0, rows - NA_KROWS))
        for ri in range(NA_QROWS):
            r = r0 + ri
            row_start = int(np.clip(r - NA_WIN_ROWS // 2, 0, rows - NA_WIN_ROWS))
            for ki in range(NA_KROWS):
                kr = start + ki
                if row_start <= kr < row_start + NA_WIN_ROWS:
                    idx[cls, ri, ki] = kr - r + (NA_WIN_ROWS - 1)
    blocks = jnp.take(t, jnp.asarray(idx.reshape(-1)), axis=1)
    blocks = blocks.reshape(h, 3, NA_QROWS, NA_KROWS, GRID_W, GRID_W)
    blocks = blocks.transpose(1, 0, 2, 4, 3, 5)
    return blocks.reshape(3, h, NA_QROWS * GRID_W, NA_KROWS * GRID_W)


def _rope_tables(seq):
    t = jnp.arange(seq)
    row = (t // GRID_W).astype(F32)
    col = (t % GRID_W).astype(F32)
    axis_dim = HEAD_DIM // 2
    inv_freq = ROPE_THETA ** (-jnp.arange(0, axis_dim, 2, dtype=F32) / axis_dim)
    ar = row[:, None] * inv_freq
    ac = col[:, None] * inv_freq
    cos = jnp.concatenate([jnp.cos(ar), jnp.cos(ac), jnp.cos(ar), jnp.cos(ac)], axis=1)
    sin = jnp.concatenate([-jnp.sin(ar), -jnp.sin(ac), jnp.sin(ar), jnp.sin(ac)], axis=1)
    return cos, sin


def _rope_layout(a, n_heads):
    nf = HEAD_DIM // 4
    lead = a.shape[:-1]
    n = n_heads * HEAD_DIM
    p = a[..., :n].reshape(*lead, n_heads, 2, 2, nf)
    p = jnp.swapaxes(p, -3, -2).reshape(*lead, n)
    return jnp.concatenate([p, a[..., n:]], axis=-1)


def _gain_rows(g):
    g = _rope_layout(g, 1)
    return jnp.stack([g, jnp.roll(g, HEAD_DIM // 2)])


def kernel(x, ln_g, ln_b, ffn_w_in, ffn_w_out, ga_w_qkv, ga_q_norm, ga_k_norm, ga_w_o,
           na_w_qkv, na_rpb, na_w_o):
    bsz, s, d = x.shape
    m = bsz * s
    rows = s // GRID_W
    cos, sin = _rope_tables(s)

    def ln_p(i, k):
        return ln_g[i, k][None, :], ln_b[i, k][None, :]

    w_in = ffn_w_in.astype(BF16)
    w_out = ffn_w_out.astype(BF16)

    def ffn(h, i, k):
        return _ffn_ln(h, w_in, w_out, *ln_p(i, 2 * k), i, k)

    h = x.reshape(m, d)
    for i in range(DEPTH):
        h = ffn(h, i, 0)
        j = i // 2
        if i % 2 == 0:
            qg, kg = _gain_rows(ga_q_norm[j]), _gain_rows(ga_k_norm[j])
            w_qkv = _rope_layout(ga_w_qkv[j], N_HEADS + GA_KV_HEADS).astype(BF16)
            q, k, vt = _gqa_qkv(h.reshape(bsz, s, d), w_qkv, cos, sin, qg, kg)
            o = _gqa_flash(q, k, vt, qg, kg)
            w_o = ga_w_o[j]
        else:
            qkv = _na_qkv(h.reshape(bsz, s, d), na_w_qkv[j].astype(BF16))
            o = _na_attn(qkv, _na_bias_table(na_rpb[j], rows))
            w_o = na_w_o[j]
        h = _proj_ln(o.reshape(m, d), h, w_o.astype(BF16), *ln_p(i, 1))
        h = ffn(h, i, 1)
    return h.reshape(bsz, s, d)
```

```python
import functools
import math

import jax
import jax.numpy as jnp
import numpy as np
from jax import lax
from jax.experimental import pallas as pl
from jax.experimental.pallas import tpu as pltpu

F32 = jnp.float32
BF16 = jnp.bfloat16

GRID_W = 64
HEAD_DIM = 128
N_HEADS = 16
GA_KV_HEADS = 4
GA_GROUPS = N_HEADS // GA_KV_HEADS
ROPE_THETA = 10000.0
NA_WIN_ROWS = 8
NA_WIN_COLS = 16
DEPTH = 2
DEEPNORM_ALPHA = (2 * DEPTH) ** 0.25
LN_EPS = 1e-5
QK_EPS = 1e-6
ATTN_SCALE = HEAD_DIM ** -0.5
LOG2E = math.log2(math.e)
SCORE_SCALE = ATTN_SCALE * LOG2E
NEG_BIG = -1e30
VT_ROWS = HEAD_DIM + 16
BOUND_SLACK = 1.0 + 2.0 ** -5
STABILISER_MAX = 48.0
FLASH_UNROLL = 4

VMEM_LIMIT = 56 * 1024 * 1024
VMEM_LIMIT_FFN = 62 * 1024 * 1024

NA_QROWS = 4
NA_KROWS = NA_QROWS + NA_WIN_ROWS
NA_MASKED = 2 * NA_WIN_ROWS - 1
NA_UNROLL = 8


def _layer_norm(z, g, b, z_scale=1.0):
    mu = jnp.mean(z, axis=-1, keepdims=True)
    zc = z - mu
    var = jnp.mean(zc * zc, axis=-1, keepdims=True)
    return zc * lax.rsqrt(var + (z_scale * z_scale) * LN_EPS) * g + b


def _ffn_ln_kernel(x_ref, wg_ref, wu_ref, wo_ref, g_ref, b_ref, o_ref, xb_ref, *, sub, sub_edge):
    j = pl.program_id(1)
    last = pl.num_programs(1) - 1
    tm = xb_ref.shape[0]

    def swiglu_out(xb):
        gate = jnp.dot(xb, wg_ref[...], preferred_element_type=F32)
        up = jnp.dot(xb, wu_ref[...], preferred_element_type=F32)
        h = (gate * jax.nn.sigmoid(gate) * up).astype(BF16)
        return jnp.dot(h, wo_ref[...], preferred_element_type=F32)

    @pl.when(j == 0)
    def _():
        for r in range(0, tm, sub_edge):
            xb = x_ref[r:r + sub_edge, :].astype(BF16)
            xb_ref[r:r + sub_edge, :] = xb
            o_ref[r:r + sub_edge, :] = swiglu_out(xb)

    @pl.when(jnp.logical_and(j > 0, j < last))
    def _():
        for r in range(0, tm, sub):
            o_ref[r:r + sub, :] += swiglu_out(xb_ref[r:r + sub, :])

    @pl.when(j == last)
    def _():
        def total(r):
            return o_ref[r:r + sub_edge, :] + swiglu_out(xb_ref[r:r + sub_edge, :])

        y_next = total(0)
        for r in range(0, tm, sub_edge):
            y = y_next
            if r + sub_edge < tm:
                y_next = total(r + sub_edge)
            z2 = (2.0 * DEEPNORM_ALPHA) * x_ref[r:r + sub_edge, :] + y
            o_ref[r:r + sub_edge, :] = _layer_norm(z2, g_ref[...], b_ref[...], z_scale=2.0)


def _ffn_ln(x2, w_in, w_out, g, b, layer, which, *, tm=1024, tf=512, sub=512, sub_edge=256):
    m, d = x2.shape
    f = w_out.shape[2]
    nf = f // tf
    assert m % tm == 0 and tm % sub == 0 and tm % sub_edge == 0 and f % tf == 0 and nf >= 2
    return pl.pallas_call(
        functools.partial(_ffn_ln_kernel, sub=sub, sub_edge=sub_edge),
        out_shape=jax.ShapeDtypeStruct((m, d), F32),
        grid=(m // tm, nf),
        in_specs=[
            pl.BlockSpec((tm, d), lambda i, j: (i, 0)),
            pl.BlockSpec((None, None, d, tf), lambda i, j: (layer, which, 0, j)),
            pl.BlockSpec((None, None, d, tf), lambda i, j: (layer, which, 0, j + nf)),
            pl.BlockSpec((None, None, tf, d), lambda i, j: (layer, which, j, 0)),
            pl.BlockSpec((1, d), lambda i, j: (0, 0)),
            pl.BlockSpec((1, d), lambda i, j: (0, 0)),
        ],
        out_specs=pl.BlockSpec((tm, d), lambda i, j: (i, 0)),
        scratch_shapes=[pltpu.VMEM((tm, d), BF16)],
        compiler_params=pltpu.CompilerParams(
            dimension_semantics=("parallel", "arbitrary"), vmem_limit_bytes=VMEM_LIMIT_FFN),
        name="ffn_ln",
    )(x2, w_in, w_in, w_out, g, b)


def _proj_ln_kernel(a_ref, x_ref, w_ref, g_ref, b_ref, o_ref, *, sub):
    for r in range(0, a_ref.shape[0], sub):
        y = jnp.dot(a_ref[r:r + sub, :], w_ref[...], preferred_element_type=F32)
        z = DEEPNORM_ALPHA * x_ref[r:r + sub, :] + y
        o_ref[r:r + sub, :] = _layer_norm(z, g_ref[...], b_ref[...])


def _proj_ln(a2, x2, w, g, b, *, tm=1024, sub=256):
    m, d = x2.shape
    k = a2.shape[1]
    return pl.pallas_call(
        functools.partial(_proj_ln_kernel, sub=sub),
        out_shape=jax.ShapeDtypeStruct((m, d), F32),
        grid=(m // tm,),
        in_specs=[
            pl.BlockSpec((tm, k), lambda i: (i, 0)),
            pl.BlockSpec((tm, d), lambda i: (i, 0)),
            pl.BlockSpec((k, d), lambda i: (0, 0)),
            pl.BlockSpec((1, d), lambda i: (0, 0)),
            pl.BlockSpec((1, d), lambda i: (0, 0)),
        ],
        out_specs=pl.BlockSpec((tm, d), lambda i: (i, 0)),
        compiler_params=pltpu.CompilerParams(
            dimension_semantics=("parallel",), vmem_limit_bytes=VMEM_LIMIT),
        name="proj_ln",
    )(a2, x2, w, g, b)


def _score_bound(qg, kg):
    qmax = jnp.max(jnp.abs(qg), axis=1, keepdims=True)
    kmax = jnp.max(jnp.abs(kg), axis=1, keepdims=True)
    return (SCORE_SCALE * HEAD_DIM * BOUND_SLACK) * qmax * kmax


def _gqa_qkv_kernel(x_ref, w_ref, cos_ref, sin_ref, qg_ref, kg_ref, q_ref, k_ref, vt_ref,
                    *, heads_per_dot):
    tm = x_ref.shape[1]
    xb = x_ref[0].astype(BF16)
    lane0 = lax.broadcasted_iota(jnp.int32, (tm, HEAD_DIM), 1) == 0
    cos = cos_ref[...]
    sin = sin_ref[...]
    q_cos = cos * (qg_ref[0:1, :] * SCORE_SCALE)
    q_sin = sin * (qg_ref[1:2, :] * SCORE_SCALE)
    k_cos = cos * kg_ref[0:1, :]
    k_sin = sin * kg_ref[1:2, :]

    def norm_rope(yh, c, s):
        ms = jnp.mean(yh * yh, axis=-1, keepdims=True)
        yn = yh * lax.rsqrt(ms + QK_EPS)
        return yn * c + pltpu.roll(yn, HEAD_DIM // 2, 1) * s

    q_hi = jnp.where(lane0, -_score_bound(qg_ref[0:1, :], kg_ref[0:1, :]), 0.0).astype(BF16)
    k_hi = jnp.where(lane0, 1.0, 0.0).astype(BF16)
    ones = jnp.ones((VT_ROWS - HEAD_DIM, tm), BF16)

    n_all = N_HEADS + 2 * GA_KV_HEADS

    def project(h0):
        c0 = h0 * HEAD_DIM
        return jnp.dot(xb, w_ref[:, c0:c0 + heads_per_dot * HEAD_DIM], preferred_element_type=F32)

    y_next = project(0)
    for h0 in range(0, n_all, heads_per_dot):
        y = y_next
        if h0 + heads_per_dot < n_all:
            y_next = project(h0 + heads_per_dot)
        for t in range(heads_per_dot):
            h = h0 + t
            yh = y[:, t * HEAD_DIM:(t + 1) * HEAD_DIM]
            if h < N_HEADS:
                q_ref[0, h, :, :HEAD_DIM] = norm_rope(yh, q_cos, q_sin).astype(BF16)
                q_ref[0, h, :, HEAD_DIM:] = q_hi
            elif h < N_HEADS + GA_KV_HEADS:
                k_ref[0, h - N_HEADS, :, :HEAD_DIM] = norm_rope(yh, k_cos, k_sin).astype(BF16)
                k_ref[0, h - N_HEADS, :, HEAD_DIM:] = k_hi
            else:
                hv = h - N_HEADS - GA_KV_HEADS
                vt_ref[0, hv, 0, :HEAD_DIM, :] = yh.T.astype(BF16)
                vt_ref[0, hv, 0, HEAD_DIM:, :] = ones


def _gqa_qkv(x3, w, cos, sin, qg, kg, *, tm=512, heads_per_dot=4):
    bsz, s, d = x3.shape
    n = w.shape[1]
    hd = HEAD_DIM
    return pl.pallas_call(
        functools.partial(_gqa_qkv_kernel, heads_per_dot=heads_per_dot),
        out_shape=(
            jax.ShapeDtypeStruct((bsz, N_HEADS, s, 2 * hd), BF16),
            jax.ShapeDtypeStruct((bsz, GA_KV_HEADS, s, 2 * hd), BF16),
            jax.ShapeDtypeStruct((bsz, GA_KV_HEADS, s // tm, VT_ROWS, tm), BF16),
        ),
        grid=(bsz, s // tm),
        in_specs=[
            pl.BlockSpec((1, tm, d), lambda b, i: (b, i, 0)),
            pl.BlockSpec((d, n), lambda b, i: (0, 0)),
            pl.BlockSpec((tm, hd), lambda b, i: (i, 0)),
            pl.BlockSpec((tm, hd), lambda b, i: (i, 0)),
            pl.BlockSpec((2, hd), lambda b, i: (0, 0)),
            pl.BlockSpec((2, hd), lambda b, i: (0, 0)),
        ],
        out_specs=(
            pl.BlockSpec((1, N_HEADS, tm, 2 * hd), lambda b, i: (b, 0, i, 0)),
            pl.BlockSpec((1, GA_KV_HEADS, tm, 2 * hd), lambda b, i: (b, 0, i, 0)),
            pl.BlockSpec((1, GA_KV_HEADS, 1, VT_ROWS, tm), lambda b, i: (b, 0, i, 0, 0)),
        ),
        compiler_params=pltpu.CompilerParams(
            dimension_semantics=("parallel", "parallel"), vmem_limit_bytes=VMEM_LIMIT),
        name="gqa_qkv",
    )(x3, w, cos, sin, qg, kg)


def _gqa_flash_kernel(q_ref, k_ref, vt_ref, qg_ref, kg_ref, o_ref, qt_ref, mx_ref, acc_ref, st_ref,
                      *, tq, tk):
    hd = HEAD_DIM
    nk = k_ref.shape[2] // tk
    for h in range(GA_GROUPS):
        qt_ref[h] = q_ref[0, h].T

    @pl.when(jnp.max(_score_bound(qg_ref[0:1, :], kg_ref[0:1, :])) > STABILISER_MAX)
    def _():
        row0 = lax.broadcasted_iota(jnp.int32, (16, tq), 0) == 0
        for h in range(GA_GROUPS):
            mx_ref[...] = jnp.full(mx_ref.shape, NEG_BIG, F32)

            def max_body(i, carry):
                off = pl.multiple_of(i * tk, tk)
                s = jnp.dot(k_ref[0, 0, pl.ds(off, tk), :hd], qt_ref[h, :hd, :],
                            preferred_element_type=F32)
                mx_ref[...] = jnp.maximum(mx_ref[...], jnp.max(s, axis=0, keepdims=True))
                return carry

            lax.fori_loop(0, nk, max_body, 0)
            qt_ref[h, hd:hd + 16, :] = jnp.where(row0, -mx_ref[...], 0.0).astype(BF16)

    acc_ref[...] = jnp.zeros(acc_ref.shape, F32)

    def scores(i, h):
        off = pl.multiple_of(i * tk, tk)
        return jnp.dot(k_ref[0, 0, pl.ds(off, tk), :], qt_ref[h],
                       preferred_element_type=F32)

    st_ref[...] = scores(0, 0)

    def body(it, carry):
        st = st_ref[...]
        for u in range(FLASH_UNROLL):
            i = it * FLASH_UNROLL + u
            vt = vt_ref[0, 0, i]
            for h in range(GA_GROUPS):
                if h + 1 < GA_GROUPS:
                    st_next = scores(i, h + 1)
                else:
                    st_next = scores(jnp.minimum(i + 1, nk - 1), 0)
                p = jnp.exp2(st).astype(BF16)
                acc_ref[h] += jnp.dot(vt, p, preferred_element_type=F32)
                st = st_next
        st_ref[...] = st
        return carry

    lax.fori_loop(0, nk // FLASH_UNROLL, body, 0)
    for h in range(GA_GROUPS):
        out_t = acc_ref[h, :hd, :] / acc_ref[h, hd:hd + 1, :]
        o_ref[0, :, h * hd:(h + 1) * hd] = out_t.T.astype(BF16)


def _gqa_flash(q, k, vt, qg, kg, *, tq=512):
    bsz, _, s, hd2 = q.shape
    hd = hd2 // 2
    nk, tk = vt.shape[2], vt.shape[4]
    assert s % tq == 0 and nk % FLASH_UNROLL == 0
    return pl.pallas_call(
        functools.partial(_gqa_flash_kernel, tq=tq, tk=tk),
        out_shape=jax.ShapeDtypeStruct((bsz, s, N_HEADS * hd), BF16),
        grid=(bsz, GA_KV_HEADS, s // tq),
        in_specs=[
            pl.BlockSpec((1, GA_GROUPS, tq, hd2), lambda b, g, i: (b, g, i, 0)),
            pl.BlockSpec((1, 1, s, hd2), lambda b, g, i: (b, g, 0, 0)),
            pl.BlockSpec((1, 1, nk, VT_ROWS, tk), lambda b, g, i: (b, g, 0, 0, 0)),
            pl.BlockSpec((2, hd), lambda b, g, i: (0, 0)),
            pl.BlockSpec((2, hd), lambda b, g, i: (0, 0)),
        ],
        out_specs=pl.BlockSpec((1, tq, GA_GROUPS * hd), lambda b, g, i: (b, i, g)),
        scratch_shapes=[pltpu.VMEM((GA_GROUPS, hd2, tq), BF16), pltpu.VMEM((1, tq), F32),
                        pltpu.VMEM((GA_GROUPS, VT_ROWS, tq), F32), pltpu.VMEM((tk, tq), F32)],
        compiler_params=pltpu.CompilerParams(
            dimension_semantics=("parallel", "parallel", "parallel"), vmem_limit_bytes=VMEM_LIMIT),
        name="gqa_flash",
    )(q, k, vt, qg, kg)


def _na_qkv_kernel(x_ref, w_ref, o_ref, *, q_tiles):
    xb = x_ref[0].astype(BF16)
    y = jnp.dot(xb, w_ref[...], preferred_element_type=F32)
    y = y * jnp.where(pl.program_id(2) < q_tiles, SCORE_SCALE, 1.0)
    for t in range(o_ref.shape[1]):
        o_ref[0, t] = y[:, t * HEAD_DIM:(t + 1) * HEAD_DIM].astype(BF16)


def _na_qkv(x3, w, *, tm=1024, tn=1024):
    bsz, s, d = x3.shape
    n = w.shape[1]
    hd = HEAD_DIM
    return pl.pallas_call(
        functools.partial(_na_qkv_kernel, q_tiles=N_HEADS * hd // tn),
        out_shape=jax.ShapeDtypeStruct((bsz, n // hd, s, hd), BF16),
        grid=(bsz, s // tm, n // tn),
        in_specs=[
            pl.BlockSpec((1, tm, d), lambda b, i, j: (b, i, 0)),
            pl.BlockSpec((d, tn), lambda b, i, j: (0, j)),
        ],
        out_specs=pl.BlockSpec((1, tn // hd, tm, hd), lambda b, i, j: (b, j, i, 0)),
        compiler_params=pltpu.CompilerParams(
            dimension_semantics=("parallel", "parallel", "arbitrary"), vmem_limit_bytes=VMEM_LIMIT),
        name="na_qkv",
    )(x3, w)


def _na_block_offsets(rows):
    nblk = rows // NA_QROWS
    idx = np.full((3, NA_QROWS, NA_KROWS), NA_MASKED, np.int32)
    for cls, rb in enumerate((0, 1, nblk - 1)):
        r0 = rb * NA_QROWS
        start = int(np.clip(r0 - NA_WIN_ROWS // 2, 0, rows - NA_KROWS))
        for ri in range(NA_QROWS):
            r = r0 + ri
            row_start = int(np.clip(r - NA_WIN_ROWS // 2, 0, rows - NA_WIN_ROWS))
            for ki in range(NA_KROWS):
                kr = start + ki
                if row_start <= kr < row_start + NA_WIN_ROWS:
                    idx[cls, ri, ki] = kr - r + (NA_WIN_ROWS - 1)
    return idx


def _na_attn_kernel(q_ref, k_ref, v_ref, t_ref, o_ref, va_ref, bias_ref, *, rows):
    hd = HEAD_DIM
    nq = NA_QROWS * GRID_W
    nkeys = NA_KROWS * GRID_W
    nblk = rows // NA_QROWS

    @pl.when(pl.program_id(1) == 0)
    def _():
        idx = _na_block_offsets(rows)
        for cls in range(3):
            for ri in range(NA_QROWS):
                for kp in range(NA_KROWS // 2):
                    d0, d1 = int(idx[cls, ri, 2 * kp]), int(idx[cls, ri, 2 * kp + 1])
                    if d0 == NA_MASKED and d1 == NA_MASKED:
                        tile = jnp.full((GRID_W, 2 * GRID_W), NEG_BIG, F32)
                    else:
                        tile = jnp.concatenate([t_ref[0, d0], t_ref[0, d1]], axis=1)
                    bias_ref[cls, ri * GRID_W:(ri + 1) * GRID_W,
                             kp * 2 * GRID_W:(kp + 1) * 2 * GRID_W] = tile

    va_ref[:, :hd] = v_ref[0, 0]
    va_ref[:, hd:] = jnp.ones((va_ref.shape[0], hd), BF16)

    def block_scores(rb):
        qoff = pl.multiple_of(rb * nq, nq)
        start = jnp.clip(rb * NA_QROWS - NA_WIN_ROWS // 2, 0, rows - NA_KROWS)
        koff = pl.multiple_of(start * GRID_W, GRID_W)
        cls = jnp.where(rb == 0, 0, jnp.where(rb == nblk - 1, 2, 1))
        q = q_ref[0, 0, pl.ds(qoff, nq), :]
        k = k_ref[0, 0, pl.ds(koff, nkeys), :]
        s = lax.dot_general(q, k, (((1,), (1,)), ((), ())), preferred_element_type=F32)
        return qoff, koff, s + bias_ref[cls]

    def body(it, carry):
        items = [block_scores(it * NA_UNROLL + u) for u in range(NA_UNROLL)]
        for qoff, koff, s in items:
            m = jnp.max(s, axis=1, keepdims=True)
            p = jnp.exp2(s - m).astype(BF16)
            oa = jnp.dot(p, va_ref[pl.ds(koff, nkeys), :], preferred_element_type=F32)
            o_ref[0, pl.ds(qoff, nq), :] = (oa[:, :hd] / oa[:, hd:hd + 1]).astype(BF16)
        return carry

    lax.fori_loop(0, nblk // NA_UNROLL, body, 0)


def _na_attn(qkv, t):
    bsz, _, s, hd = qkv.shape
    rows = s // GRID_W
    nq = NA_QROWS * GRID_W
    nkeys = NA_KROWS * GRID_W
    assert rows >= NA_KROWS and rows % (NA_QROWS * NA_UNROLL) == 0 and NA_KROWS % 2 == 0
    return pl.pallas_call(
        functools.partial(_na_attn_kernel, rows=rows),
        out_shape=jax.ShapeDtypeStruct((bsz, s, N_HEADS * hd), BF16),
        grid=(N_HEADS, bsz),
        in_specs=[
            pl.BlockSpec((1, 1, s, hd), lambda h, b: (b, h, 0, 0)),
            pl.BlockSpec((1, 1, s, hd), lambda h, b: (b, N_HEADS + h, 0, 0)),
            pl.BlockSpec((1, 1, s, hd), lambda h, b: (b, 2 * N_HEADS + h, 0, 0)),
            pl.BlockSpec((1, NA_MASKED + 1, GRID_W, GRID_W), lambda h, b: (h, 0, 0, 0)),
        ],
        out_specs=pl.BlockSpec((1, s, hd), lambda h, b: (b, 0, h)),
        scratch_shapes=[pltpu.VMEM((s, 2 * hd), BF16), pltpu.VMEM((3, nq, nkeys), F32)],
        compiler_params=pltpu.CompilerParams(
            dimension_semantics=("parallel", "arbitrary"), vmem_limit_bytes=VMEM_LIMIT),
        name="na_attn",
    )(qkv, qkv, qkv, t)


def _na_bias_tiles(rpb):
    h = rpb.shape[0]
    c = np.arange(GRID_W)
    col_start = np.clip(c - NA_WIN_COLS // 2, 0, GRID_W - NA_WIN_COLS)
    kc = np.arange(GRID_W)
    col_rel = kc[None, :] - c[:, None] + (NA_WIN_COLS - 1)
    col_ok = (kc[None, :] >= col_start[:, None]) & (kc[None, :] < col_start[:, None] + NA_WIN_COLS)
    col_rel = np.clip(col_rel, 0, 2 * NA_WIN_COLS - 2)
    t = jnp.where(jnp.asarray(col_ok)[None, None], rpb[:, :, jnp.asarray(col_rel)] * LOG2E, NEG_BIG)
    return jnp.concatenate([t, jnp.full((h, 1, GRID_W, GRID_W), NEG_BIG, F32)], axis=1)


def _rope_tables(seq):
    t = jnp.arange(seq)
    row = (t // GRID_W).astype(F32)
    col = (t % GRID_W).astype(F32)
    axis_dim = HEAD_DIM // 2
    inv_freq = ROPE_THETA ** (-jnp.arange(0, axis_dim, 2, dtype=F32) / axis_dim)
    ar = row[:, None] * inv_freq
    ac = col[:, None] * inv_freq
    cos = jnp.concatenate([jnp.cos(ar), jnp.cos(ac), jnp.cos(ar), jnp.cos(ac)], axis=1)
    sin = jnp.concatenate([-jnp.sin(ar), -jnp.sin(ac), jnp.sin(ar), jnp.sin(ac)], axis=1)
    return cos, sin


def _rope_layout(a, n_heads):
    nf = HEAD_DIM // 4
    lead = a.shape[:-1]
    n = n_heads * HEAD_DIM
    p = a[..., :n].reshape(*lead, n_heads, 2, 2, nf)
    p = jnp.swapaxes(p, -3, -2).reshape(*lead, n)
    return jnp.concatenate([p, a[..., n:]], axis=-1)


def _gain_rows(g):
    g = _rope_layout(g, 1)
    return jnp.stack([g, jnp.roll(g, HEAD_DIM // 2)])


def kernel(x, ln_g, ln_b, ffn_w_in, ffn_w_out, ga_w_qkv, ga_q_norm, ga_k_norm, ga_w_o,
           na_w_qkv, na_rpb, na_w_o):
    bsz, s, d = x.shape
    m = bsz * s
    cos, sin = _rope_tables(s)

    def ln_p(i, k):
        return ln_g[i, k][None, :], ln_b[i, k][None, :]

    w_in = ffn_w_in.astype(BF16)
    w_out = ffn_w_out.astype(BF16)

    def ffn(h, i, k):
        return _ffn_ln(h, w_in, w_out, *ln_p(i, 2 * k), i, k)

    h = x.reshape(m, d)
    for i in range(DEPTH):
        h = ffn(h, i, 0)
        j = i // 2
        if i % 2 == 0:
            qg, kg = _gain_rows(ga_q_norm[j]), _gain_rows(ga_k_norm[j])
            w_qkv = _rope_layout(ga_w_qkv[j], N_HEADS + GA_KV_HEADS).astype(BF16)
            q, k, vt = _gqa_qkv(h.reshape(bsz, s, d), w_qkv, cos, sin, qg, kg)
            o = _gqa_flash(q, k, vt, qg, kg)
            w_o = ga_w_o[j]
        else:
            qkv = _na_qkv(h.reshape(bsz, s, d), na_w_qkv[j].astype(BF16))
            o = _na_attn(qkv, _na_bias_tiles(na_rpb[j]))
            w_o = na_w_o[j]
        h = _proj_ln(o.reshape(m, d), h, w_o.astype(BF16), *ln_p(i, 1))
        h = ffn(h, i, 1)
    return h.reshape(bsz, s, d)
```

```python
import functools
import math

import jax
import jax.numpy as jnp
import numpy as np
from jax import lax
from jax.experimental import pallas as pl
from jax.experimental.pallas import tpu as pltpu

F32 = jnp.float32
BF16 = jnp.bfloat16

GRID_W = 64
HEAD_DIM = 128
N_HEADS = 16
GA_KV_HEADS = 4
GA_GROUPS = N_HEADS // GA_KV_HEADS
ROPE_THETA = 10000.0
NA_WIN_ROWS = 8
NA_WIN_COLS = 16
DEPTH = 2
DEEPNORM_ALPHA = (2 * DEPTH) ** 0.25
LN_EPS = 1e-5
QK_EPS = 1e-6
ATTN_SCALE = HEAD_DIM ** -0.5
LOG2E = math.log2(math.e)
SCORE_SCALE = ATTN_SCALE * LOG2E
NEG_BIG = -1e30
VT_ROWS = HEAD_DIM + 16
BOUND_SLACK = 1.0 + 2.0 ** -5
STABILISER_MAX = 48.0
FLASH_UNROLL = 4

VMEM_LIMIT = 56 * 1024 * 1024
VMEM_LIMIT_FFN = 62 * 1024 * 1024

NA_QROWS = 4
NA_KROWS = NA_QROWS + NA_WIN_ROWS
NA_MASKED = 2 * NA_WIN_ROWS - 1
NA_UNROLL = 8


def _layer_norm(z, g, b, z_scale=1.0):
    mu = jnp.mean(z, axis=-1, keepdims=True)
    zc = z - mu
    var = jnp.mean(zc * zc, axis=-1, keepdims=True)
    return zc * lax.rsqrt(var + (z_scale * z_scale) * LN_EPS) * g + b


def _ffn_ln_kernel(x_ref, wg_ref, wu_ref, wo_ref, g_ref, b_ref, o_ref, xb_ref, *, sub, sub_edge):
    j = pl.program_id(1)
    last = pl.num_programs(1) - 1
    tm = xb_ref.shape[0]

    def swiglu_out(xb):
        gate = jnp.dot(xb, wg_ref[...], preferred_element_type=F32)
        up = jnp.dot(xb, wu_ref[...], preferred_element_type=F32)
        h = (gate * jax.nn.sigmoid(gate) * up).astype(BF16)
        return jnp.dot(h, wo_ref[...], preferred_element_type=F32)

    @pl.when(j == 0)
    def _():
        for r in range(0, tm, sub_edge):
            xb = x_ref[r:r + sub_edge, :].astype(BF16)
            xb_ref[r:r + sub_edge, :] = xb
            o_ref[r:r + sub_edge, :] = swiglu_out(xb)

    @pl.when(jnp.logical_and(j > 0, j < last))
    def _():
        for r in range(0, tm, sub):
            o_ref[r:r + sub, :] += swiglu_out(xb_ref[r:r + sub, :])

    @pl.when(j == last)
    def _():
        def total(r):
            return o_ref[r:r + sub_edge, :] + swiglu_out(xb_ref[r:r + sub_edge, :])

        y_next = total(0)
        for r in range(0, tm, sub_edge):
            y = y_next
            if r + sub_edge < tm:
                y_next = total(r + sub_edge)
            z2 = (2.0 * DEEPNORM_ALPHA) * x_ref[r:r + sub_edge, :] + y
            o_ref[r:r + sub_edge, :] = _layer_norm(z2, g_ref[...], b_ref[...], z_scale=2.0)


FFN_CHUNK = 512


def _ffn_chunked(w_in):
    *lead, d, n = w_in.shape
    w = w_in.reshape(*lead, d, n // FFN_CHUNK, FFN_CHUNK)
    return jnp.swapaxes(w, -3, -2)


def _ffn_ln(x2, w_in, w_out, g, b, layer, which, *, tm=1024, sub=512, sub_edge=256):
    m, d = x2.shape
    f = w_out.shape[2]
    tf = FFN_CHUNK
    nf = f // tf
    assert m % tm == 0 and tm % sub == 0 and tm % sub_edge == 0 and f % tf == 0 and nf >= 2
    return pl.pallas_call(
        functools.partial(_ffn_ln_kernel, sub=sub, sub_edge=sub_edge),
        out_shape=jax.ShapeDtypeStruct((m, d), F32),
        grid=(m // tm, nf),
        in_specs=[
            pl.BlockSpec((tm, d), lambda i, j: (i, 0)),
            pl.BlockSpec((None, None, None, d, tf), lambda i, j: (layer, which, j, 0, 0)),
            pl.BlockSpec((None, None, None, d, tf), lambda i, j: (layer, which, j + nf, 0, 0)),
            pl.BlockSpec((None, None, tf, d), lambda i, j: (layer, which, j, 0)),
            pl.BlockSpec((1, d), lambda i, j: (0, 0)),
            pl.BlockSpec((1, d), lambda i, j: (0, 0)),
        ],
        out_specs=pl.BlockSpec((tm, d), lambda i, j: (i, 0)),
        scratch_shapes=[pltpu.VMEM((tm, d), BF16)],
        compiler_params=pltpu.CompilerParams(
            dimension_semantics=("parallel", "arbitrary"), vmem_limit_bytes=VMEM_LIMIT_FFN),
        name="ffn_ln",
    )(x2, w_in, w_in, w_out, g, b)


def _proj_ln_kernel(a_ref, x_ref, w_ref, g_ref, b_ref, o_ref, *, sub):
    for r in range(0, a_ref.shape[0], sub):
        y = jnp.dot(a_ref[r:r + sub, :], w_ref[...], preferred_element_type=F32)
        z = DEEPNORM_ALPHA * x_ref[r:r + sub, :] + y
        o_ref[r:r + sub, :] = _layer_norm(z, g_ref[...], b_ref[...])


def _proj_ln(a2, x2, w, g, b, *, tm=1024, sub=256):
    m, d = x2.shape
    k = a2.shape[1]
    return pl.pallas_call(
        functools.partial(_proj_ln_kernel, sub=sub),
        out_shape=jax.ShapeDtypeStruct((m, d), F32),
        grid=(m // tm,),
        in_specs=[
            pl.BlockSpec((tm, k), lambda i: (i, 0)),
            pl.BlockSpec((tm, d), lambda i: (i, 0)),
            pl.BlockSpec((k, d), lambda i: (0, 0)),
            pl.BlockSpec((1, d), lambda i: (0, 0)),
            pl.BlockSpec((1, d), lambda i: (0, 0)),
        ],
        out_specs=pl.BlockSpec((tm, d), lambda i: (i, 0)),
        compiler_params=pltpu.CompilerParams(
            dimension_semantics=("parallel",), vmem_limit_bytes=VMEM_LIMIT),
        name="proj_ln",
    )(a2, x2, w, g, b)


def _score_bound(qg, kg):
    qmax = jnp.max(jnp.abs(qg), axis=1, keepdims=True)
    kmax = jnp.max(jnp.abs(kg), axis=1, keepdims=True)
    return (SCORE_SCALE * HEAD_DIM * BOUND_SLACK) * qmax * kmax


def _gqa_qkv_kernel(x_ref, w_ref, cos_ref, sin_ref, qg_ref, kg_ref, q_ref, k_ref, vt_ref,
                    *, heads_per_dot):
    tm = x_ref.shape[1]
    xb = x_ref[0].astype(BF16)
    lane0 = lax.broadcasted_iota(jnp.int32, (tm, HEAD_DIM), 1) == 0
    cos = cos_ref[...]
    sin = sin_ref[...]
    q_cos = cos * (qg_ref[0:1, :] * SCORE_SCALE)
    q_sin = sin * (qg_ref[1:2, :] * SCORE_SCALE)
    k_cos = cos * kg_ref[0:1, :]
    k_sin = sin * kg_ref[1:2, :]

    def norm_rope(yh, c, s):
        ms = jnp.mean(yh * yh, axis=-1, keepdims=True)
        yn = yh * lax.rsqrt(ms + QK_EPS)
        return yn * c + pltpu.roll(yn, HEAD_DIM // 2, 1) * s

    q_hi = jnp.where(lane0, -_score_bound(qg_ref[0:1, :], kg_ref[0:1, :]), 0.0).astype(BF16)
    k_hi = jnp.where(lane0, 1.0, 0.0).astype(BF16)
    ones = jnp.ones((VT_ROWS - HEAD_DIM, tm), BF16)

    n_all = N_HEADS + 2 * GA_KV_HEADS

    def project(h0):
        c0 = h0 * HEAD_DIM
        return jnp.dot(xb, w_ref[:, c0:c0 + heads_per_dot * HEAD_DIM], preferred_element_type=F32)

    y_next = project(0)
    for h0 in range(0, n_all, heads_per_dot):
        y = y_next
        if h0 + heads_per_dot < n_all:
            y_next = project(h0 + heads_per_dot)
        for t in range(heads_per_dot):
            h = h0 + t
            yh = y[:, t * HEAD_DIM:(t + 1) * HEAD_DIM]
            if h < N_HEADS:
                q_ref[0, h, :, :HEAD_DIM] = norm_rope(yh, q_cos, q_sin).astype(BF16)
                q_ref[0, h, :, HEAD_DIM:] = q_hi
            elif h < N_HEADS + GA_KV_HEADS:
                k_ref[0, h - N_HEADS, :, :HEAD_DIM] = norm_rope(yh, k_cos, k_sin).astype(BF16)
                k_ref[0, h - N_HEADS, :, HEAD_DIM:] = k_hi
            else:
                hv = h - N_HEADS - GA_KV_HEADS
                vt_ref[0, hv, 0, :HEAD_DIM, :] = yh.T.astype(BF16)
                vt_ref[0, hv, 0, HEAD_DIM:, :] = ones


def _gqa_qkv(x3, w, cos, sin, qg, kg, *, tm=512, heads_per_dot=4):
    bsz, s, d = x3.shape
    n = w.shape[1]
    hd = HEAD_DIM
    return pl.pallas_call(
        functools.partial(_gqa_qkv_kernel, heads_per_dot=heads_per_dot),
        out_shape=(
            jax.ShapeDtypeStruct((bsz, N_HEADS, s, 2 * hd), BF16),
            jax.ShapeDtypeStruct((bsz, GA_KV_HEADS, s, 2 * hd), BF16),
            jax.ShapeDtypeStruct((bsz, GA_KV_HEADS, s // tm, VT_ROWS, tm), BF16),
        ),
        grid=(bsz, s // tm),
        in_specs=[
            pl.BlockSpec((1, tm, d), lambda b, i: (b, i, 0)),
            pl.BlockSpec((d, n), lambda b, i: (0, 0)),
            pl.BlockSpec((tm, hd), lambda b, i: (i, 0)),
            pl.BlockSpec((tm, hd), lambda b, i: (i, 0)),
            pl.BlockSpec((2, hd), lambda b, i: (0, 0)),
            pl.BlockSpec((2, hd), lambda b, i: (0, 0)),
        ],
        out_specs=(
            pl.BlockSpec((1, N_HEADS, tm, 2 * hd), lambda b, i: (b, 0, i, 0)),
            pl.BlockSpec((1, GA_KV_HEADS, tm, 2 * hd), lambda b, i: (b, 0, i, 0)),
            pl.BlockSpec((1, GA_KV_HEADS, 1, VT_ROWS, tm), lambda b, i: (b, 0, i, 0, 0)),
        ),
        compiler_params=pltpu.CompilerParams(
            dimension_semantics=("parallel", "parallel"), vmem_limit_bytes=VMEM_LIMIT),
        name="gqa_qkv",
    )(x3, w, cos, sin, qg, kg)


def _gqa_flash_kernel(q_ref, k_ref, vt_ref, qg_ref, kg_ref, o_ref, qt_ref, mx_ref, acc_ref, st_ref,
                      *, tq, tk):
    hd = HEAD_DIM
    nk = k_ref.shape[2] // tk
    for h in range(GA_GROUPS):
        qt_ref[h] = q_ref[0, h].T

    @pl.when(jnp.max(_score_bound(qg_ref[0:1, :], kg_ref[0:1, :])) > STABILISER_MAX)
    def _():
        row0 = lax.broadcasted_iota(jnp.int32, (16, tq), 0) == 0
        for h in range(GA_GROUPS):
            mx_ref[...] = jnp.full(mx_ref.shape, NEG_BIG, F32)

            def max_body(i, carry):
                off = pl.multiple_of(i * tk, tk)
                s = jnp.dot(k_ref[0, 0, pl.ds(off, tk), :hd], qt_ref[h, :hd, :],
                            preferred_element_type=F32)
                mx_ref[...] = jnp.maximum(mx_ref[...], jnp.max(s, axis=0, keepdims=True))
                return carry

            lax.fori_loop(0, nk, max_body, 0)
            qt_ref[h, hd:hd + 16, :] = jnp.where(row0, -mx_ref[...], 0.0).astype(BF16)

    acc_ref[...] = jnp.zeros(acc_ref.shape, F32)

    def scores(i, h):
        off = pl.multiple_of(i * tk, tk)
        return jnp.dot(k_ref[0, 0, pl.ds(off, tk), :], qt_ref[h],
                       preferred_element_type=F32)

    st_ref[...] = scores(0, 0)

    def body(it, carry):
        st = st_ref[...]
        for u in range(FLASH_UNROLL):
            i = it * FLASH_UNROLL + u
            vt = vt_ref[0, 0, i]
            for h in range(GA_GROUPS):
                if h + 1 < GA_GROUPS:
                    st_next = scores(i, h + 1)
                else:
                    st_next = scores(jnp.minimum(i + 1, nk - 1), 0)
                p = jnp.exp2(st).astype(BF16)
                acc_ref[h] += jnp.dot(vt, p, preferred_element_type=F32)
                st = st_next
        st_ref[...] = st
        return carry

    lax.fori_loop(0, nk // FLASH_UNROLL, body, 0)
    for h in range(GA_GROUPS):
        out_t = acc_ref[h, :hd, :] / acc_ref[h, hd:hd + 1, :]
        o_ref[0, :, h * hd:(h + 1) * hd] = out_t.T.astype(BF16)


def _gqa_flash(q, k, vt, qg, kg, *, tq=512):
    bsz, _, s, hd2 = q.shape
    hd = hd2 // 2
    nk, tk = vt.shape[2], vt.shape[4]
    assert s % tq == 0 and nk % FLASH_UNROLL == 0
    return pl.pallas_call(
        functools.partial(_gqa_flash_kernel, tq=tq, tk=tk),
        out_shape=jax.ShapeDtypeStruct((bsz, s, N_HEADS * hd), BF16),
        grid=(bsz, GA_KV_HEADS, s // tq),
        in_specs=[
            pl.BlockSpec((1, GA_GROUPS, tq, hd2), lambda b, g, i: (b, g, i, 0)),
            pl.BlockSpec((1, 1, s, hd2), lambda b, g, i: (b, g, 0, 0)),
            pl.BlockSpec((1, 1, nk, VT_ROWS, tk), lambda b, g, i: (b, g, 0, 0, 0)),
            pl.BlockSpec((2, hd), lambda b, g, i: (0, 0)),
            pl.BlockSpec((2, hd), lambda b, g, i: (0, 0)),
        ],
        out_specs=pl.BlockSpec((1, tq, GA_GROUPS * hd), lambda b, g, i: (b, i, g)),
        scratch_shapes=[pltpu.VMEM((GA_GROUPS, hd2, tq), BF16), pltpu.VMEM((1, tq), F32),
                        pltpu.VMEM((GA_GROUPS, VT_ROWS, tq), F32), pltpu.VMEM((tk, tq), F32)],
        compiler_params=pltpu.CompilerParams(
            dimension_semantics=("parallel", "parallel", "parallel"), vmem_limit_bytes=VMEM_LIMIT),
        name="gqa_flash",
    )(q, k, vt, qg, kg)


def _na_qkv_kernel(x_ref, w_ref, o_ref, *, q_tiles):
    xb = x_ref[0].astype(BF16)
    y = jnp.dot(xb, w_ref[...], preferred_element_type=F32)
    y = y * jnp.where(pl.program_id(2) < q_tiles, SCORE_SCALE, 1.0)
    for t in range(o_ref.shape[1]):
        o_ref[0, t] = y[:, t * HEAD_DIM:(t + 1) * HEAD_DIM].astype(BF16)


def _na_qkv(x3, w, *, tm=1024, tn=1024):
    bsz, s, d = x3.shape
    n = w.shape[1]
    hd = HEAD_DIM
    return pl.pallas_call(
        functools.partial(_na_qkv_kernel, q_tiles=N_HEADS * hd // tn),
        out_shape=jax.ShapeDtypeStruct((bsz, n // hd, s, hd), BF16),
        grid=(bsz, s // tm, n // tn),
        in_specs=[
            pl.BlockSpec((1, tm, d), lambda b, i, j: (b, i, 0)),
            pl.BlockSpec((d, tn), lambda b, i, j: (0, j)),
        ],
        out_specs=pl.BlockSpec((1, tn // hd, tm, hd), lambda b, i, j: (b, j, i, 0)),
        compiler_params=pltpu.CompilerParams(
            dimension_semantics=("parallel", "parallel", "arbitrary"), vmem_limit_bytes=VMEM_LIMIT),
        name="na_qkv",
    )(x3, w)


def _na_block_offsets(rows):
    nblk = rows // NA_QROWS
    idx = np.full((3, NA_QROWS, NA_KROWS), NA_MASKED, np.int32)
    for cls, rb in enumerate((0, 1, nblk - 1)):
        r0 = rb * NA_QROWS
        start = int(np.clip(r0 - NA_WIN_ROWS // 2, 0, rows - NA_KROWS))
        for ri in range(NA_QROWS):
            r = r0 + ri
            row_start = int(np.clip(r - NA_WIN_ROWS // 2, 0, rows - NA_WIN_ROWS))
            for ki in range(NA_KROWS):
                kr = start + ki
                if row_start <= kr < row_start + NA_WIN_ROWS:
                    idx[cls, ri, ki] = kr - r + (NA_WIN_ROWS - 1)
    return idx


def _na_attn_kernel(q_ref, k_ref, v_ref, t_ref, o_ref, va_ref, bias_ref, *, rows):
    hd = HEAD_DIM
    nq = NA_QROWS * GRID_W
    nkeys = NA_KROWS * GRID_W
    nblk = rows // NA_QROWS

    @pl.when(pl.program_id(1) == 0)
    def _():
        idx = _na_block_offsets(rows)
        for cls in range(3):
            for ri in range(NA_QROWS):
                for kp in range(NA_KROWS // 2):
                    d0, d1 = int(idx[cls, ri, 2 * kp]), int(idx[cls, ri, 2 * kp + 1])
                    if d0 == NA_MASKED and d1 == NA_MASKED:
                        tile = jnp.full((GRID_W, 2 * GRID_W), NEG_BIG, F32)
                    else:
                        tile = jnp.concatenate([t_ref[0, d0], t_ref[0, d1]], axis=1)
                    bias_ref[cls, ri * GRID_W:(ri + 1) * GRID_W,
                             kp * 2 * GRID_W:(kp + 1) * 2 * GRID_W] = tile

    va_ref[:, :hd] = v_ref[0, 0]
    va_ref[:, hd:] = jnp.ones((va_ref.shape[0], hd), BF16)

    def block_scores(rb):
        qoff = pl.multiple_of(rb * nq, nq)
        start = jnp.clip(rb * NA_QROWS - NA_WIN_ROWS // 2, 0, rows - NA_KROWS)
        koff = pl.multiple_of(start * GRID_W, GRID_W)
        cls = jnp.where(rb == 0, 0, jnp.where(rb == nblk - 1, 2, 1))
        q = q_ref[0, 0, pl.ds(qoff, nq), :]
        k = k_ref[0, 0, pl.ds(koff, nkeys), :]
        s = lax.dot_general(q, k, (((1,), (1,)), ((), ())), preferred_element_type=F32)
        return qoff, koff, s + bias_ref[cls]

    def body(it, carry):
        items = [block_scores(it * NA_UNROLL + u) for u in range(NA_UNROLL)]
        for qoff, koff, s in items:
            m = jnp.max(s, axis=1, keepdims=True)
            p = jnp.exp2(s - m).astype(BF16)
            oa = jnp.dot(p, va_ref[pl.ds(koff, nkeys), :], preferred_element_type=F32)
            o_ref[0, pl.ds(qoff, nq), :] = (oa[:, :hd] / oa[:, hd:hd + 1]).astype(BF16)
        return carry

    lax.fori_loop(0, nblk // NA_UNROLL, body, 0)


def _na_attn(qkv, t):
    bsz, _, s, hd = qkv.shape
    rows = s // GRID_W
    nq = NA_QROWS * GRID_W
    nkeys = NA_KROWS * GRID_W
    assert rows >= NA_KROWS and rows % (NA_QROWS * NA_UNROLL) == 0 and NA_KROWS % 2 == 0
    return pl.pallas_call(
        functools.partial(_na_attn_kernel, rows=rows),
        out_shape=jax.ShapeDtypeStruct((bsz, s, N_HEADS * hd), BF16),
        grid=(N_HEADS, bsz),
        in_specs=[
            pl.BlockSpec((1, 1, s, hd), lambda h, b: (b, h, 0, 0)),
            pl.BlockSpec((1, 1, s, hd), lambda h, b: (b, N_HEADS + h, 0, 0)),
            pl.BlockSpec((1, 1, s, hd), lambda h, b: (b, 2 * N_HEADS + h, 0, 0)),
            pl.BlockSpec((1, NA_MASKED + 1, GRID_W, GRID_W), lambda h, b: (h, 0, 0, 0)),
        ],
        out_specs=pl.BlockSpec((1, s, hd), lambda h, b: (b, 0, h)),
        scratch_shapes=[pltpu.VMEM((s, 2 * hd), BF16), pltpu.VMEM((3, nq, nkeys), F32)],
        compiler_params=pltpu.CompilerParams(
            dimension_semantics=("parallel", "arbitrary"), vmem_limit_bytes=VMEM_LIMIT),
        name="na_attn",
    )(qkv, qkv, qkv, t)


def _na_bias_tiles(rpb):
    h = rpb.shape[0]
    c = np.arange(GRID_W)
    col_start = np.clip(c - NA_WIN_COLS // 2, 0, GRID_W - NA_WIN_COLS)
    kc = np.arange(GRID_W)
    col_rel = kc[None, :] - c[:, None] + (NA_WIN_COLS - 1)
    col_ok = (kc[None, :] >= col_start[:, None]) & (kc[None, :] < col_start[:, None] + NA_WIN_COLS)
    col_rel = np.clip(col_rel, 0, 2 * NA_WIN_COLS - 2)
    t = jnp.where(jnp.asarray(col_ok)[None, None], rpb[:, :, jnp.asarray(col_rel)] * LOG2E, NEG_BIG)
    return jnp.concatenate([t, jnp.full((h, 1, GRID_W, GRID_W), NEG_BIG, F32)], axis=1)


def _rope_tables(seq):
    t = jnp.arange(seq)
    row = (t // GRID_W).astype(F32)
    col = (t % GRID_W).astype(F32)
    axis_dim = HEAD_DIM // 2
    inv_freq = ROPE_THETA ** (-jnp.arange(0, axis_dim, 2, dtype=F32) / axis_dim)
    ar = row[:, None] * inv_freq
    ac = col[:, None] * inv_freq
    cos = jnp.concatenate([jnp.cos(ar), jnp.cos(ac), jnp.cos(ar), jnp.cos(ac)], axis=1)
    sin = jnp.concatenate([-jnp.sin(ar), -jnp.sin(ac), jnp.sin(ar), jnp.sin(ac)], axis=1)
    return cos, sin


def _rope_layout(a, n_heads):
    nf = HEAD_DIM // 4
    lead = a.shape[:-1]
    n = n_heads * HEAD_DIM
    p = a[..., :n].reshape(*lead, n_heads, 2, 2, nf)
    p = jnp.swapaxes(p, -3, -2).reshape(*lead, n)
    return jnp.concatenate([p, a[..., n:]], axis=-1)


def _gain_rows(g):
    g = _rope_layout(g, 1)
    return jnp.stack([g, jnp.roll(g, HEAD_DIM // 2)])


def kernel(x, ln_g, ln_b, ffn_w_in, ffn_w_out, ga_w_qkv, ga_q_norm, ga_k_norm, ga_w_o,
           na_w_qkv, na_rpb, na_w_o):
    bsz, s, d = x.shape
    m = bsz * s
    cos, sin = _rope_tables(s)

    def ln_p(i, k):
        return ln_g[i, k][None, :], ln_b[i, k][None, :]

    w_in = _ffn_chunked(ffn_w_in.astype(BF16))
    w_out = ffn_w_out.astype(BF16)

    def ffn(h, i, k):
        return _ffn_ln(h, w_in, w_out, *ln_p(i, 2 * k), i, k)

    h = x.reshape(m, d)
    for i in range(DEPTH):
        h = ffn(h, i, 0)
        j = i // 2
        if i % 2 == 0:
            qg, kg = _gain_rows(ga_q_norm[j]), _gain_rows(ga_k_norm[j])
            w_qkv = _rope_layout(ga_w_qkv[j], N_HEADS + GA_KV_HEADS).astype(BF16)
            q, k, vt = _gqa_qkv(h.reshape(bsz, s, d), w_qkv, cos, sin, qg, kg)
            o = _gqa_flash(q, k, vt, qg, kg)
            w_o = ga_w_o[j]
        else:
            qkv = _na_qkv(h.reshape(bsz, s, d), na_w_qkv[j].astype(BF16))
            o = _na_attn(qkv, _na_bias_tiles(na_rpb[j]))
            w_o = na_w_o[j]
        h = _proj_ln(o.reshape(m, d), h, w_o.astype(BF16), *ln_p(i, 1))
        h = ffn(h, i, 1)
    return h.reshape(bsz, s, d)
```

```python
import functools
import math

import jax
import jax.numpy as jnp
import numpy as np
from jax import lax
from jax.experimental import pallas as pl
from jax.experimental.pallas import tpu as pltpu

F32 = jnp.float32
BF16 = jnp.bfloat16

GRID_W = 64
HEAD_DIM = 128
N_HEADS = 16
GA_KV_HEADS = 4
GA_GROUPS = N_HEADS // GA_KV_HEADS
ROPE_THETA = 10000.0
NA_WIN_ROWS = 8
NA_WIN_COLS = 16
DEPTH = 2
DEEPNORM_ALPHA = (2 * DEPTH) ** 0.25
LN_EPS = 1e-5
QK_EPS = 1e-6
ATTN_SCALE = HEAD_DIM ** -0.5
LOG2E = math.log2(math.e)
SCORE_SCALE = ATTN_SCALE * LOG2E
NEG_BIG = -1e30
VT_ROWS = HEAD_DIM + 16
BOUND_SLACK = 1.0 + 2.0 ** -5
STABILISER_MAX = 48.0
FLASH_UNROLL = 4

VMEM_LIMIT = 56 * 1024 * 1024
VMEM_LIMIT_FFN = 62 * 1024 * 1024

NA_QROWS = 4
NA_KROWS = NA_QROWS + NA_WIN_ROWS
NA_MASKED = 2 * NA_WIN_ROWS - 1
NA_UNROLL = 8


def _layer_norm(z, g, b, z_scale=1.0):
    mu = jnp.mean(z, axis=-1, keepdims=True)
    zc = z - mu
    var = jnp.mean(zc * zc, axis=-1, keepdims=True)
    return zc * lax.rsqrt(var + (z_scale * z_scale) * LN_EPS) * g + b


def _ffn_ln_kernel(x_ref, wg_ref, wu_ref, wo_ref, g_ref, b_ref, o_ref, xb_ref, *, sub, sub_edge):
    j = pl.program_id(1)
    last = pl.num_programs(1) - 1
    tm = xb_ref.shape[0]

    def swiglu_out(xb):
        gate = jnp.dot(xb, wg_ref[...], preferred_element_type=F32)
        up = jnp.dot(xb, wu_ref[...], preferred_element_type=F32)
        h = (gate * jax.nn.sigmoid(gate) * up).astype(BF16)
        return jnp.dot(h, wo_ref[...], preferred_element_type=F32)

    @pl.when(j == 0)
    def _():
        for r in range(0, tm, sub):
            xb = x_ref[r:r + sub, :].astype(BF16)
            xb_ref[r:r + sub, :] = xb
            o_ref[r:r + sub, :] = swiglu_out(xb)

    @pl.when(jnp.logical_and(j > 0, j < last))
    def _():
        for r in range(0, tm, sub):
            o_ref[r:r + sub, :] += swiglu_out(xb_ref[r:r + sub, :])

    @pl.when(j == last)
    def _():
        def total(r):
            return o_ref[r:r + sub_edge, :] + swiglu_out(xb_ref[r:r + sub_edge, :])

        y_next = total(0)
        for r in range(0, tm, sub_edge):
            y = y_next
            if r + sub_edge < tm:
                y_next = total(r + sub_edge)
            z2 = (2.0 * DEEPNORM_ALPHA) * x_ref[r:r + sub_edge, :] + y
            o_ref[r:r + sub_edge, :] = _layer_norm(z2, g_ref[...], b_ref[...], z_scale=2.0)


def _ffn_ln(x2, w_in, w_out, g, b, layer, which, *, tm=1024, tf=512, sub=512, sub_edge=256):
    m, d = x2.shape
    f = w_out.shape[2]
    nf = f // tf
    assert m % tm == 0 and tm % sub == 0 and tm % sub_edge == 0 and f % tf == 0 and nf >= 2
    return pl.pallas_call(
        functools.partial(_ffn_ln_kernel, sub=sub, sub_edge=sub_edge),
        out_shape=jax.ShapeDtypeStruct((m, d), F32),
        grid=(m // tm, nf),
        in_specs=[
            pl.BlockSpec((tm, d), lambda i, j: (i, 0)),
            pl.BlockSpec((None, None, d, tf), lambda i, j: (layer, which, 0, j)),
            pl.BlockSpec((None, None, d, tf), lambda i, j: (layer, which, 0, j + nf)),
            pl.BlockSpec((None, None, tf, d), lambda i, j: (layer, which, j, 0)),
            pl.BlockSpec((1, d), lambda i, j: (0, 0)),
            pl.BlockSpec((1, d), lambda i, j: (0, 0)),
        ],
        out_specs=pl.BlockSpec((tm, d), lambda i, j: (i, 0)),
        scratch_shapes=[pltpu.VMEM((tm, d), BF16)],
        compiler_params=pltpu.CompilerParams(
            dimension_semantics=("parallel", "arbitrary"), vmem_limit_bytes=VMEM_LIMIT_FFN),
        name="ffn_ln",
    )(x2, w_in, w_in, w_out, g, b)


def _proj_ln_kernel(a_ref, x_ref, w_ref, g_ref, b_ref, o_ref, *, sub):
    for r in range(0, a_ref.shape[0], sub):
        y = jnp.dot(a_ref[r:r + sub, :], w_ref[...], preferred_element_type=F32)
        z = DEEPNORM_ALPHA * x_ref[r:r + sub, :] + y
        o_ref[r:r + sub, :] = _layer_norm(z, g_ref[...], b_ref[...])


def _proj_ln(a2, x2, w, g, b, *, tm=1024, sub=256):
    m, d = x2.shape
    k = a2.shape[1]
    return pl.pallas_call(
        functools.partial(_proj_ln_kernel, sub=sub),
        out_shape=jax.ShapeDtypeStruct((m, d), F32),
        grid=(m // tm,),
        in_specs=[
            pl.BlockSpec((tm, k), lambda i: (i, 0)),
            pl.BlockSpec((tm, d), lambda i: (i, 0)),
            pl.BlockSpec((k, d), lambda i: (0, 0)),
            pl.BlockSpec((1, d), lambda i: (0, 0)),
            pl.BlockSpec((1, d), lambda i: (0, 0)),
        ],
        out_specs=pl.BlockSpec((tm, d), lambda i: (i, 0)),
        compiler_params=pltpu.CompilerParams(
            dimension_semantics=("parallel",), vmem_limit_bytes=VMEM_LIMIT),
        name="proj_ln",
    )(a2, x2, w, g, b)


def _score_bound(qg, kg):
    qmax = jnp.max(jnp.abs(qg), axis=1, keepdims=True)
    kmax = jnp.max(jnp.abs(kg), axis=1, keepdims=True)
    return (SCORE_SCALE * HEAD_DIM * BOUND_SLACK) * qmax * kmax


def _gqa_qkv_kernel(x_ref, w_ref, cos_ref, sin_ref, qg_ref, kg_ref, q_ref, k_ref, vt_ref,
                    *, heads_per_dot):
    tm = x_ref.shape[1]
    xb = x_ref[0].astype(BF16)
    lane0 = lax.broadcasted_iota(jnp.int32, (tm, HEAD_DIM), 1) == 0
    cos = cos_ref[...]
    sin = sin_ref[...]
    q_cos = cos * (qg_ref[0:1, :] * SCORE_SCALE)
    q_sin = sin * (qg_ref[1:2, :] * SCORE_SCALE)
    k_cos = cos * kg_ref[0:1, :]
    k_sin = sin * kg_ref[1:2, :]

    def norm_rope(yh, c, s):
        ms = jnp.mean(yh * yh, axis=-1, keepdims=True)
        yn = yh * lax.rsqrt(ms + QK_EPS)
        return yn * c + pltpu.roll(yn, HEAD_DIM // 2, 1) * s

    q_hi = jnp.where(lane0, -_score_bound(qg_ref[0:1, :], kg_ref[0:1, :]), 0.0).astype(BF16)
    k_hi = jnp.where(lane0, 1.0, 0.0).astype(BF16)
    ones = jnp.ones((VT_ROWS - HEAD_DIM, tm), BF16)

    n_all = N_HEADS + 2 * GA_KV_HEADS

    def project(h0):
        c0 = h0 * HEAD_DIM
        return jnp.dot(xb, w_ref[:, c0:c0 + heads_per_dot * HEAD_DIM], preferred_element_type=F32)

    y_next = project(0)
    for h0 in range(0, n_all, heads_per_dot):
        y = y_next
        if h0 + heads_per_dot < n_all:
            y_next = project(h0 + heads_per_dot)
        for t in range(heads_per_dot):
            h = h0 + t
            yh = y[:, t * HEAD_DIM:(t + 1) * HEAD_DIM]
            if h < N_HEADS:
                q_ref[0, h, :, :HEAD_DIM] = norm_rope(yh, q_cos, q_sin).astype(BF16)
                q_ref[0, h, :, HEAD_DIM:] = q_hi
            elif h < N_HEADS + GA_KV_HEADS:
                k_ref[0, h - N_HEADS, :, :HEAD_DIM] = norm_rope(yh, k_cos, k_sin).astype(BF16)
                k_ref[0, h - N_HEADS, :, HEAD_DIM:] = k_hi
            else:
                hv = h - N_HEADS - GA_KV_HEADS
                vt_ref[0, hv, 0, :HEAD_DIM, :] = yh.T.astype(BF16)
                vt_ref[0, hv, 0, HEAD_DIM:, :] = ones


def _gqa_qkv(x3, w, cos, sin, qg, kg, *, tm=512, heads_per_dot=4):
    bsz, s, d = x3.shape
    n = w.shape[1]
    hd = HEAD_DIM
    return pl.pallas_call(
        functools.partial(_gqa_qkv_kernel, heads_per_dot=heads_per_dot),
        out_shape=(
            jax.ShapeDtypeStruct((bsz, N_HEADS, s, 2 * hd), BF16),
            jax.ShapeDtypeStruct((bsz, GA_KV_HEADS, s, 2 * hd), BF16),
            jax.ShapeDtypeStruct((bsz, GA_KV_HEADS, s // tm, VT_ROWS, tm), BF16),
        ),
        grid=(bsz, s // tm),
        in_specs=[
            pl.BlockSpec((1, tm, d), lambda b, i: (b, i, 0)),
            pl.BlockSpec((d, n), lambda b, i: (0, 0)),
            pl.BlockSpec((tm, hd), lambda b, i: (i, 0)),
            pl.BlockSpec((tm, hd), lambda b, i: (i, 0)),
            pl.BlockSpec((2, hd), lambda b, i: (0, 0)),
            pl.BlockSpec((2, hd), lambda b, i: (0, 0)),
        ],
        out_specs=(
            pl.BlockSpec((1, N_HEADS, tm, 2 * hd), lambda b, i: (b, 0, i, 0)),
            pl.BlockSpec((1, GA_KV_HEADS, tm, 2 * hd), lambda b, i: (b, 0, i, 0)),
            pl.BlockSpec((1, GA_KV_HEADS, 1, VT_ROWS, tm), lambda b, i: (b, 0, i, 0, 0)),
        ),
        compiler_params=pltpu.CompilerParams(
            dimension_semantics=("parallel", "parallel"), vmem_limit_bytes=VMEM_LIMIT),
        name="gqa_qkv",
    )(x3, w, cos, sin, qg, kg)


def _gqa_flash_kernel(q_ref, k_ref, vt_ref, qg_ref, kg_ref, o_ref, qt_ref, mx_ref, acc_ref, st_ref,
                      *, tq, tk):
    hd = HEAD_DIM
    nk = k_ref.shape[2] // tk
    exact_max = jnp.max(_score_bound(qg_ref[0:1, :], kg_ref[0:1, :])) > STABILISER_MAX
    mx_ref[...] = jnp.zeros(mx_ref.shape, BF16)

    @pl.when(exact_max)
    def _():
        row0 = lax.broadcasted_iota(jnp.int32, (16, tq), 0) == 0
        for h in range(GA_GROUPS):
            def max_body(i, m):
                off = pl.multiple_of(i * tk, tk)
                s = lax.dot_general(k_ref[0, 0, pl.ds(off, tk), :hd], q_ref[0, h, :, :hd],
                                    (((1,), (1,)), ((), ())), preferred_element_type=F32)
                return jnp.maximum(m, jnp.max(s, axis=0, keepdims=True))

            m = lax.fori_loop(0, nk, max_body, jnp.full((1, tq), NEG_BIG, F32))
            mx_ref[h] = jnp.where(row0, -m, 0.0).astype(BF16)

    for h in range(GA_GROUPS):
        qt = q_ref[0, h].T
        qt_ref[h] = qt
        qt_ref[h, hd:hd + 16, :] = jnp.where(exact_max, mx_ref[h], qt[hd:hd + 16, :])
    acc_ref[...] = jnp.zeros(acc_ref.shape, F32)

    def scores(i, h):
        off = pl.multiple_of(i * tk, tk)
        return jnp.dot(k_ref[0, 0, pl.ds(off, tk), :], qt_ref[h],
                       preferred_element_type=F32)

    st_ref[...] = scores(0, 0)

    def body(it, carry):
        st = st_ref[...]
        for u in range(FLASH_UNROLL):
            i = it * FLASH_UNROLL + u
            vt = vt_ref[0, 0, i]
            for h in range(GA_GROUPS):
                if h + 1 < GA_GROUPS:
                    st_next = scores(i, h + 1)
                else:
                    st_next = scores(jnp.minimum(i + 1, nk - 1), 0)
                p = jnp.exp2(st).astype(BF16)
                acc_ref[h] += jnp.dot(vt, p, preferred_element_type=F32)
                st = st_next
        st_ref[...] = st
        return carry

    lax.fori_loop(0, nk // FLASH_UNROLL, body, 0)
    for h in range(GA_GROUPS):
        out_t = acc_ref[h, :hd, :] / acc_ref[h, hd:hd + 1, :]
        o_ref[0, :, h * hd:(h + 1) * hd] = out_t.T.astype(BF16)


def _gqa_flash(q, k, vt, qg, kg, *, tq=512):
    bsz, _, s, hd2 = q.shape
    hd = hd2 // 2
    nk, tk = vt.shape[2], vt.shape[4]
    assert s % tq == 0 and nk % FLASH_UNROLL == 0
    return pl.pallas_call(
        functools.partial(_gqa_flash_kernel, tq=tq, tk=tk),
        out_shape=jax.ShapeDtypeStruct((bsz, s, N_HEADS * hd), BF16),
        grid=(bsz, GA_KV_HEADS, s // tq),
        in_specs=[
            pl.BlockSpec((1, GA_GROUPS, tq, hd2), lambda b, g, i: (b, g, i, 0)),
            pl.BlockSpec((1, 1, s, hd2), lambda b, g, i: (b, g, 0, 0)),
            pl.BlockSpec((1, 1, nk, VT_ROWS, tk), lambda b, g, i: (b, g, 0, 0, 0)),
            pl.BlockSpec((2, hd), lambda b, g, i: (0, 0)),
            pl.BlockSpec((2, hd), lambda b, g, i: (0, 0)),
        ],
        out_specs=pl.BlockSpec((1, tq, GA_GROUPS * hd), lambda b, g, i: (b, i, g)),
        scratch_shapes=[pltpu.VMEM((GA_GROUPS, hd2, tq), BF16), pltpu.VMEM((GA_GROUPS, 16, tq), BF16),
                        pltpu.VMEM((GA_GROUPS, VT_ROWS, tq), F32), pltpu.VMEM((tk, tq), F32)],
        compiler_params=pltpu.CompilerParams(
            dimension_semantics=("parallel", "parallel", "parallel"), vmem_limit_bytes=VMEM_LIMIT),
        name="gqa_flash",
    )(q, k, vt, qg, kg)


def _na_qkv_kernel(x_ref, w_ref, o_ref, *, q_tiles):
    xb = x_ref[0].astype(BF16)
    y = jnp.dot(xb, w_ref[...], preferred_element_type=F32)
    y = y * jnp.where(pl.program_id(2) < q_tiles, SCORE_SCALE, 1.0)
    for t in range(o_ref.shape[1]):
        o_ref[0, t] = y[:, t * HEAD_DIM:(t + 1) * HEAD_DIM].astype(BF16)


def _na_qkv(x3, w, *, tm=1024, tn=1024):
    bsz, s, d = x3.shape
    n = w.shape[1]
    hd = HEAD_DIM
    return pl.pallas_call(
        functools.partial(_na_qkv_kernel, q_tiles=N_HEADS * hd // tn),
        out_shape=jax.ShapeDtypeStruct((bsz, n // hd, s, hd), BF16),
        grid=(bsz, s // tm, n // tn),
        in_specs=[
            pl.BlockSpec((1, tm, d), lambda b, i, j: (b, i, 0)),
            pl.BlockSpec((d, tn), lambda b, i, j: (0, j)),
        ],
        out_specs=pl.BlockSpec((1, tn // hd, tm, hd), lambda b, i, j: (b, j, i, 0)),
        compiler_params=pltpu.CompilerParams(
            dimension_semantics=("parallel", "parallel", "arbitrary"), vmem_limit_bytes=VMEM_LIMIT),
        name="na_qkv",
    )(x3, w)


def _na_block_offsets(rows):
    nblk = rows // NA_QROWS
    idx = np.full((3, NA_QROWS, NA_KROWS), NA_MASKED, np.int32)
    for cls, rb in enumerate((0, 1, nblk - 1)):
        r0 = rb * NA_QROWS
        start = int(np.clip(r0 - NA_WIN_ROWS // 2, 0, rows - NA_KROWS))
        for ri in range(NA_QROWS):
            r = r0 + ri
            row_start = int(np.clip(r - NA_WIN_ROWS // 2, 0, rows - NA_WIN_ROWS))
            for ki in range(NA_KROWS):
                kr = start + ki
                if row_start <= kr < row_start + NA_WIN_ROWS:
                    idx[cls, ri, ki] = kr - r + (NA_WIN_ROWS - 1)
    return idx


def _na_attn_kernel(q_ref, k_ref, v_ref, t_ref, o_ref, va_ref, bias_ref, *, rows):
    hd = HEAD_DIM
    nq = NA_QROWS * GRID_W
    nkeys = NA_KROWS * GRID_W
    nblk = rows // NA_QROWS

    @pl.when(pl.program_id(1) == 0)
    def _():
        idx = _na_block_offsets(rows)
        for cls in range(3):
            for ri in range(NA_QROWS):
                for kp in range(NA_KROWS // 2):
                    d0, d1 = int(idx[cls, ri, 2 * kp]), int(idx[cls, ri, 2 * kp + 1])
                    if d0 == NA_MASKED and d1 == NA_MASKED:
                        tile = jnp.full((GRID_W, 2 * GRID_W), NEG_BIG, F32)
                    else:
                        tile = jnp.concatenate([t_ref[0, d0], t_ref[0, d1]], axis=1)
                    bias_ref[cls, ri * GRID_W:(ri + 1) * GRID_W,
                             kp * 2 * GRID_W:(kp + 1) * 2 * GRID_W] = tile

    va_ref[:, :hd] = v_ref[0, 0]
    va_ref[:, hd:] = jnp.ones((va_ref.shape[0], hd), BF16)

    def block_scores(rb):
        qoff = pl.multiple_of(rb * nq, nq)
        start = jnp.clip(rb * NA_QROWS - NA_WIN_ROWS // 2, 0, rows - NA_KROWS)
        koff = pl.multiple_of(start * GRID_W, GRID_W)
        cls = jnp.where(rb == 0, 0, jnp.where(rb == nblk - 1, 2, 1))
        q = q_ref[0, 0, pl.ds(qoff, nq), :]
        k = k_ref[0, 0, pl.ds(koff, nkeys), :]
        s = lax.dot_general(q, k, (((1,), (1,)), ((), ())), preferred_element_type=F32)
        return qoff, koff, s + bias_ref[cls]

    def body(it, carry):
        items = [block_scores(it * NA_UNROLL + u) for u in range(NA_UNROLL)]
        for qoff, koff, s in items:
            m = jnp.max(s, axis=1, keepdims=True)
            p = jnp.exp2(s - m).astype(BF16)
            oa = jnp.dot(p, va_ref[pl.ds(koff, nkeys), :], preferred_element_type=F32)
            o_ref[0, pl.ds(qoff, nq), :] = (oa[:, :hd] / oa[:, hd:hd + 1]).astype(BF16)
        return carry

    lax.fori_loop(0, nblk // NA_UNROLL, body, 0)


def _na_attn(qkv, t):
    bsz, _, s, hd = qkv.shape
    rows = s // GRID_W
    nq = NA_QROWS * GRID_W
    nkeys = NA_KROWS * GRID_W
    assert rows >= NA_KROWS and rows % (NA_QROWS * NA_UNROLL) == 0 and NA_KROWS % 2 == 0
    return pl.pallas_call(
        functools.partial(_na_attn_kernel, rows=rows),
        out_shape=jax.ShapeDtypeStruct((bsz, s, N_HEADS * hd), BF16),
        grid=(N_HEADS, bsz),
        in_specs=[
            pl.BlockSpec((1, 1, s, hd), lambda h, b: (b, h, 0, 0)),
            pl.BlockSpec((1, 1, s, hd), lambda h, b: (b, N_HEADS + h, 0, 0)),
            pl.BlockSpec((1, 1, s, hd), lambda h, b: (b, 2 * N_HEADS + h, 0, 0)),
            pl.BlockSpec((1, NA_MASKED + 1, GRID_W, GRID_W), lambda h, b: (h, 0, 0, 0)),
        ],
        out_specs=pl.BlockSpec((1, s, hd), lambda h, b: (b, 0, h)),
        scratch_shapes=[pltpu.VMEM((s, 2 * hd), BF16), pltpu.VMEM((3, nq, nkeys), F32)],
        compiler_params=pltpu.CompilerParams(
            dimension_semantics=("parallel", "arbitrary"), vmem_limit_bytes=VMEM_LIMIT),
        name="na_attn",
    )(qkv, qkv, qkv, t)


def _na_bias_tiles(rpb):
    h = rpb.shape[0]
    c = np.arange(GRID_W)
    col_start = np.clip(c - NA_WIN_COLS // 2, 0, GRID_W - NA_WIN_COLS)
    kc = np.arange(GRID_W)
    col_ok = (kc[None, :] >= col_start[:, None]) & (kc[None, :] < col_start[:, None] + NA_WIN_COLS)
    pad = GRID_W - NA_WIN_COLS
    padded = jnp.pad(rpb * LOG2E, ((0, 0), (0, 0), (pad, pad)))
    t = jnp.stack([padded[:, :, GRID_W - 1 - ci: 2 * GRID_W - 1 - ci] for ci in range(GRID_W)], axis=2)
    t = jnp.where(jnp.asarray(col_ok)[None, None], t, NEG_BIG)
    return jnp.concatenate([t, jnp.full((h, 1, GRID_W, GRID_W), NEG_BIG, F32)], axis=1)


def _rope_tables(seq):
    t = np.arange(seq)
    row = (t // GRID_W).astype(np.float32)
    col = (t % GRID_W).astype(np.float32)
    axis_dim = HEAD_DIM // 2
    inv_freq = np.float32(ROPE_THETA) ** (-np.arange(0, axis_dim, 2, dtype=np.float32) / np.float32(axis_dim))
    ar = row[:, None] * inv_freq
    ac = col[:, None] * inv_freq
    cos = np.concatenate([np.cos(ar), np.cos(ac), np.cos(ar), np.cos(ac)], axis=1)
    sin = np.concatenate([-np.sin(ar), -np.sin(ac), np.sin(ar), np.sin(ac)], axis=1)
    return jnp.asarray(cos, F32), jnp.asarray(sin, F32)


def _rope_layout(a, n_heads):
    nf = HEAD_DIM // 4
    lead = a.shape[:-1]
    n = n_heads * HEAD_DIM
    p = a[..., :n].reshape(*lead, n_heads, 2, 2, nf)
    p = jnp.swapaxes(p, -3, -2).reshape(*lead, n)
    return jnp.concatenate([p, a[..., n:]], axis=-1)


def _gain_rows(g):
    g = _rope_layout(g, 1)
    return jnp.stack([g, jnp.roll(g, HEAD_DIM // 2)])


def kernel(x, ln_g, ln_b, ffn_w_in, ffn_w_out, ga_w_qkv, ga_q_norm, ga_k_norm, ga_w_o,
           na_w_qkv, na_rpb, na_w_o):
    bsz, s, d = x.shape
    m = bsz * s
    cos, sin = _rope_tables(s)

    def ln_p(i, k):
        return ln_g[i, k][None, :], ln_b[i, k][None, :]

    w_in = ffn_w_in.astype(BF16)
    w_out = ffn_w_out.astype(BF16)

    def ffn(h, i, k):
        return _ffn_ln(h, w_in, w_out, *ln_p(i, 2 * k), i, k)

    h = x.reshape(m, d)
    for i in range(DEPTH):
        h = ffn(h, i, 0)
        j = i // 2
        if i % 2 == 0:
            qg, kg = _gain_rows(ga_q_norm[j]), _gain_rows(ga_k_norm[j])
            w_qkv = _rope_layout(ga_w_qkv[j], N_HEADS + GA_KV_HEADS).astype(BF16)
            q, k, vt = _gqa_qkv(h.reshape(bsz, s, d), w_qkv, cos, sin, qg, kg)
            o = _gqa_flash(q, k, vt, qg, kg)
            w_o = ga_w_o[j]
        else:
            qkv = _na_qkv(h.reshape(bsz, s, d), na_w_qkv[j].astype(BF16))
            o = _na_attn(qkv, _na_bias_tiles(na_rpb[j]))
            w_o = na_w_o[j]
        h = _proj_ln(o.reshape(m, d), h, w_o.astype(BF16), *ln_p(i, 1))
        h = ffn(h, i, 1)
    return h.reshape(bsz, s, d)
```

```python
import functools
import math

import jax
import jax.numpy as jnp
import numpy as np
from jax import lax
from jax.experimental import pallas as pl
from jax.experimental.pallas import tpu as pltpu

F32 = jnp.float32
BF16 = jnp.bfloat16

GRID_W = 64
HEAD_DIM = 128
N_HEADS = 16
GA_KV_HEADS = 4
GA_GROUPS = N_HEADS // GA_KV_HEADS
ROPE_THETA = 10000.0
NA_WIN_ROWS = 8
NA_WIN_COLS = 16
DEPTH = 2
DEEPNORM_ALPHA = (2 * DEPTH) ** 0.25
LN_EPS = 1e-5
QK_EPS = 1e-6
ATTN_SCALE = HEAD_DIM ** -0.5
LOG2E = math.log2(math.e)
SCORE_SCALE = ATTN_SCALE * LOG2E
NEG_BIG = -1e30
VT_ROWS = HEAD_DIM + 16
BOUND_SLACK = 1.0 + 2.0 ** -5
STABILISER_MAX = 48.0
FLASH_UNROLL = 8

VMEM_LIMIT = 56 * 1024 * 1024
VMEM_LIMIT_FFN = 62 * 1024 * 1024

NA_QROWS = 4
NA_KROWS = NA_QROWS + NA_WIN_ROWS
NA_MASKED = 2 * NA_WIN_ROWS - 1
NA_UNROLL = 16


def _layer_norm(z, g, b, z_scale=1.0):
    mu = jnp.mean(z, axis=-1, keepdims=True)
    zc = z - mu
    var = jnp.mean(zc * zc, axis=-1, keepdims=True)
    return zc * lax.rsqrt(var + (z_scale * z_scale) * LN_EPS) * g + b


def _ffn_ln_kernel(x_ref, wg_ref, wu_ref, wo_ref, g_ref, b_ref, o_ref, xb_ref, *, sub, sub_edge):
    j = pl.program_id(1)
    last = pl.num_programs(1) - 1
    tm = xb_ref.shape[0]

    def swiglu_out(xb):
        gate = jnp.dot(xb, wg_ref[...], preferred_element_type=F32)
        up = jnp.dot(xb, wu_ref[...], preferred_element_type=F32)
        h = (gate * jax.nn.sigmoid(gate) * up).astype(BF16)
        return jnp.dot(h, wo_ref[...], preferred_element_type=F32)

    @pl.when(j == 0)
    def _():
        for r in range(0, tm, sub):
            xb = x_ref[r:r + sub, :].astype(BF16)
            xb_ref[r:r + sub, :] = xb
            o_ref[r:r + sub, :] = swiglu_out(xb)

    @pl.when(jnp.logical_and(j > 0, j < last))
    def _():
        for r in range(0, tm, sub):
            o_ref[r:r + sub, :] += swiglu_out(xb_ref[r:r + sub, :])

    @pl.when(j == last)
    def _():
        def total(r):
            return o_ref[r:r + sub_edge, :] + swiglu_out(xb_ref[r:r + sub_edge, :])

        y_next = total(0)
        for r in range(0, tm, sub_edge):
            y = y_next
            if r + sub_edge < tm:
                y_next = total(r + sub_edge)
            z2 = (2.0 * DEEPNORM_ALPHA) * x_ref[r:r + sub_edge, :] + y
            o_ref[r:r + sub_edge, :] = _layer_norm(z2, g_ref[...], b_ref[...], z_scale=2.0)


def _ffn_ln(x2, w_in, w_out, g, b, layer, which, *, tm=1024, tf=512, sub=512, sub_edge=256):
    m, d = x2.shape
    f = w_out.shape[2]
    nf = f // tf
    assert m % tm == 0 and tm % sub == 0 and tm % sub_edge == 0 and f % tf == 0 and nf >= 2
    return pl.pallas_call(
        functools.partial(_ffn_ln_kernel, sub=sub, sub_edge=sub_edge),
        out_shape=jax.ShapeDtypeStruct((m, d), F32),
        grid=(m // tm, nf),
        in_specs=[
            pl.BlockSpec((tm, d), lambda i, j: (i, 0)),
            pl.BlockSpec((None, None, d, tf), lambda i, j: (layer, which, 0, j)),
            pl.BlockSpec((None, None, d, tf), lambda i, j: (layer, which, 0, j + nf)),
            pl.BlockSpec((None, None, tf, d), lambda i, j: (layer, which, j, 0)),
            pl.BlockSpec((1, d), lambda i, j: (0, 0)),
            pl.BlockSpec((1, d), lambda i, j: (0, 0)),
        ],
        out_specs=pl.BlockSpec((tm, d), lambda i, j: (i, 0)),
        scratch_shapes=[pltpu.VMEM((tm, d), BF16)],
        compiler_params=pltpu.CompilerParams(
            dimension_semantics=("parallel", "arbitrary"), vmem_limit_bytes=VMEM_LIMIT_FFN),
        name="ffn_ln",
    )(x2, w_in, w_in, w_out, g, b)


def _proj_ln_kernel(a_ref, x_ref, w_ref, g_ref, b_ref, o_ref, *, sub):
    for r in range(0, a_ref.shape[0], sub):
        y = jnp.dot(a_ref[r:r + sub, :], w_ref[...], preferred_element_type=F32)
        z = DEEPNORM_ALPHA * x_ref[r:r + sub, :] + y
        o_ref[r:r + sub, :] = _layer_norm(z, g_ref[...], b_ref[...])


def _proj_ln(a2, x2, w, g, b, *, tm=1024, sub=256):
    m, d = x2.shape
    k = a2.shape[1]
    return pl.pallas_call(
        functools.partial(_proj_ln_kernel, sub=sub),
        out_shape=jax.ShapeDtypeStruct((m, d), F32),
        grid=(m // tm,),
        in_specs=[
            pl.BlockSpec((tm, k), lambda i: (i, 0)),
            pl.BlockSpec((tm, d), lambda i: (i, 0)),
            pl.BlockSpec((k, d), lambda i: (0, 0)),
            pl.BlockSpec((1, d), lambda i: (0, 0)),
            pl.BlockSpec((1, d), lambda i: (0, 0)),
        ],
        out_specs=pl.BlockSpec((tm, d), lambda i: (i, 0)),
        compiler_params=pltpu.CompilerParams(
            dimension_semantics=("parallel",), vmem_limit_bytes=VMEM_LIMIT),
        name="proj_ln",
    )(a2, x2, w, g, b)


def _score_bound(qg, kg):
    qmax = jnp.max(jnp.abs(qg), axis=1, keepdims=True)
    kmax = jnp.max(jnp.abs(kg), axis=1, keepdims=True)
    return (SCORE_SCALE * HEAD_DIM * BOUND_SLACK) * qmax * kmax


def _gqa_qkv_kernel(x_ref, w_ref, cos_ref, sin_ref, qg_ref, kg_ref, q_ref, k_ref, vt_ref,
                    *, heads_per_dot):
    tm = x_ref.shape[1]
    xb = x_ref[0].astype(BF16)
    lane0 = lax.broadcasted_iota(jnp.int32, (tm, HEAD_DIM), 1) == 0
    cos = cos_ref[...]
    sin = sin_ref[...]
    q_cos = cos * (qg_ref[0:1, :] * SCORE_SCALE)
    q_sin = sin * (qg_ref[1:2, :] * SCORE_SCALE)
    k_cos = cos * kg_ref[0:1, :]
    k_sin = sin * kg_ref[1:2, :]

    def norm_rope(yh, c, s):
        ms = jnp.mean(yh * yh, axis=-1, keepdims=True)
        return (yh * c + pltpu.roll(yh, HEAD_DIM // 2, 1) * s) * lax.rsqrt(ms + QK_EPS)

    q_hi = jnp.where(lane0, -_score_bound(qg_ref[0:1, :], kg_ref[0:1, :]), 0.0).astype(BF16)
    k_hi = jnp.where(lane0, 1.0, 0.0).astype(BF16)
    ones = jnp.ones((VT_ROWS - HEAD_DIM, tm), BF16)

    n_all = N_HEADS + 2 * GA_KV_HEADS

    def project(h0):
        c0 = h0 * HEAD_DIM
        return jnp.dot(xb, w_ref[:, c0:c0 + heads_per_dot * HEAD_DIM], preferred_element_type=F32)

    y_next = project(0)
    for h0 in range(0, n_all, heads_per_dot):
        y = y_next
        if h0 + heads_per_dot < n_all:
            y_next = project(h0 + heads_per_dot)
        for t in range(heads_per_dot):
            h = h0 + t
            yh = y[:, t * HEAD_DIM:(t + 1) * HEAD_DIM]
            if h < N_HEADS:
                q_ref[0, h, :, :HEAD_DIM] = norm_rope(yh, q_cos, q_sin).astype(BF16)
                q_ref[0, h, :, HEAD_DIM:] = q_hi
            elif h < N_HEADS + GA_KV_HEADS:
                k_ref[0, h - N_HEADS, :, :HEAD_DIM] = norm_rope(yh, k_cos, k_sin).astype(BF16)
                k_ref[0, h - N_HEADS, :, HEAD_DIM:] = k_hi
            else:
                hv = h - N_HEADS - GA_KV_HEADS
                vt_ref[0, hv, 0, :HEAD_DIM, :] = yh.T.astype(BF16)
                vt_ref[0, hv, 0, HEAD_DIM:, :] = ones


def _gqa_qkv(x3, w, cos, sin, qg, kg, *, tm=512, heads_per_dot=4):
    bsz, s, d = x3.shape
    n = w.shape[1]
    hd = HEAD_DIM
    return pl.pallas_call(
        functools.partial(_gqa_qkv_kernel, heads_per_dot=heads_per_dot),
        out_shape=(
            jax.ShapeDtypeStruct((bsz, N_HEADS, s, 2 * hd), BF16),
            jax.ShapeDtypeStruct((bsz, GA_KV_HEADS, s, 2 * hd), BF16),
            jax.ShapeDtypeStruct((bsz, GA_KV_HEADS, s // tm, VT_ROWS, tm), BF16),
        ),
        grid=(bsz, s // tm),
        in_specs=[
            pl.BlockSpec((1, tm, d), lambda b, i: (b, i, 0)),
            pl.BlockSpec((d, n), lambda b, i: (0, 0)),
            pl.BlockSpec((tm, hd), lambda b, i: (i, 0)),
            pl.BlockSpec((tm, hd), lambda b, i: (i, 0)),
            pl.BlockSpec((2, hd), lambda b, i: (0, 0)),
            pl.BlockSpec((2, hd), lambda b, i: (0, 0)),
        ],
        out_specs=(
            pl.BlockSpec((1, N_HEADS, tm, 2 * hd), lambda b, i: (b, 0, i, 0)),
            pl.BlockSpec((1, GA_KV_HEADS, tm, 2 * hd), lambda b, i: (b, 0, i, 0)),
            pl.BlockSpec((1, GA_KV_HEADS, 1, VT_ROWS, tm), lambda b, i: (b, 0, i, 0, 0)),
        ),
        compiler_params=pltpu.CompilerParams(
            dimension_semantics=("parallel", "parallel"), vmem_limit_bytes=VMEM_LIMIT),
        name="gqa_qkv",
    )(x3, w, cos, sin, qg, kg)


def _gqa_flash_kernel(q_ref, k_ref, vt_ref, qg_ref, kg_ref, o_ref, qt_ref, mx_ref, acc_ref, st_ref,
                      *, tq, tk):
    hd = HEAD_DIM
    nk = k_ref.shape[2] // tk
    exact_max = jnp.max(_score_bound(qg_ref[0:1, :], kg_ref[0:1, :])) > STABILISER_MAX
    mx_ref[...] = jnp.zeros(mx_ref.shape, BF16)

    @pl.when(exact_max)
    def _():
        row0 = lax.broadcasted_iota(jnp.int32, (16, tq), 0) == 0
        for h in range(GA_GROUPS):
            def max_body(i, m):
                off = pl.multiple_of(i * tk, tk)
                s = lax.dot_general(k_ref[0, 0, pl.ds(off, tk), :hd], q_ref[0, h, :, :hd],
                                    (((1,), (1,)), ((), ())), preferred_element_type=F32)
                return jnp.maximum(m, jnp.max(s, axis=0, keepdims=True))

            m = lax.fori_loop(0, nk, max_body, jnp.full((1, tq), NEG_BIG, F32))
            mx_ref[h] = jnp.where(row0, -m, 0.0).astype(BF16)

    for h in range(GA_GROUPS):
        qt = q_ref[0, h].T
        qt_ref[h] = qt
        qt_ref[h, hd:hd + 16, :] = jnp.where(exact_max, mx_ref[h], qt[hd:hd + 16, :])
    acc_ref[...] = jnp.zeros(acc_ref.shape, F32)

    def scores(i, h):
        off = pl.multiple_of(i * tk, tk)
        return jnp.dot(k_ref[0, 0, pl.ds(off, tk), :], qt_ref[h],
                       preferred_element_type=F32)

    st_ref[...] = scores(0, 0)

    def body(it, carry):
        st = st_ref[...]
        for u in range(FLASH_UNROLL):
            i = it * FLASH_UNROLL + u
            vt = vt_ref[0, 0, i]
            for h in range(GA_GROUPS):
                if h + 1 < GA_GROUPS:
                    st_next = scores(i, h + 1)
                else:
                    st_next = scores(jnp.minimum(i + 1, nk - 1), 0)
                p = jnp.exp2(st).astype(BF16)
                acc_ref[h] += jnp.dot(vt, p, preferred_element_type=F32)
                st = st_next
        st_ref[...] = st
        return carry

    lax.fori_loop(0, nk // FLASH_UNROLL, body, 0)
    for h in range(GA_GROUPS):
        out_t = acc_ref[h, :hd, :] / acc_ref[h, hd:hd + 1, :]
        o_ref[0, :, h * hd:(h + 1) * hd] = out_t.T.astype(BF16)


def _gqa_flash(q, k, vt, qg, kg, *, tq=512):
    bsz, _, s, hd2 = q.shape
    hd = hd2 // 2
    nk, tk = vt.shape[2], vt.shape[4]
    assert s % tq == 0 and nk % FLASH_UNROLL == 0
    return pl.pallas_call(
        functools.partial(_gqa_flash_kernel, tq=tq, tk=tk),
        out_shape=jax.ShapeDtypeStruct((bsz, s, N_HEADS * hd), BF16),
        grid=(bsz, GA_KV_HEADS, s // tq),
        in_specs=[
            pl.BlockSpec((1, GA_GROUPS, tq, hd2), lambda b, g, i: (b, g, i, 0)),
            pl.BlockSpec((1, 1, s, hd2), lambda b, g, i: (b, g, 0, 0)),
            pl.BlockSpec((1, 1, nk, VT_ROWS, tk), lambda b, g, i: (b, g, 0, 0, 0)),
            pl.BlockSpec((2, hd), lambda b, g, i: (0, 0)),
            pl.BlockSpec((2, hd), lambda b, g, i: (0, 0)),
        ],
        out_specs=pl.BlockSpec((1, tq, GA_GROUPS * hd), lambda b, g, i: (b, i, g)),
        scratch_shapes=[pltpu.VMEM((GA_GROUPS, hd2, tq), BF16), pltpu.VMEM((GA_GROUPS, 16, tq), BF16),
                        pltpu.VMEM((GA_GROUPS, VT_ROWS, tq), F32), pltpu.VMEM((tk, tq), F32)],
        compiler_params=pltpu.CompilerParams(
            dimension_semantics=("parallel", "parallel", "parallel"), vmem_limit_bytes=VMEM_LIMIT),
        name="gqa_flash",
    )(q, k, vt, qg, kg)


def _na_qkv_kernel(x_ref, w_ref, o_ref, *, q_tiles):
    xb = x_ref[0].astype(BF16)
    y = jnp.dot(xb, w_ref[...], preferred_element_type=F32)
    y = y * jnp.where(pl.program_id(2) < q_tiles, SCORE_SCALE, 1.0)
    for t in range(o_ref.shape[1]):
        o_ref[0, t] = y[:, t * HEAD_DIM:(t + 1) * HEAD_DIM].astype(BF16)


def _na_qkv(x3, w, *, tm=1024, tn=1024):
    bsz, s, d = x3.shape
    n = w.shape[1]
    hd = HEAD_DIM
    return pl.pallas_call(
        functools.partial(_na_qkv_kernel, q_tiles=N_HEADS * hd // tn),
        out_shape=jax.ShapeDtypeStruct((bsz, n // hd, s, hd), BF16),
        grid=(bsz, s // tm, n // tn),
        in_specs=[
            pl.BlockSpec((1, tm, d), lambda b, i, j: (b, i, 0)),
            pl.BlockSpec((d, tn), lambda b, i, j: (0, j)),
        ],
        out_specs=pl.BlockSpec((1, tn // hd, tm, hd), lambda b, i, j: (b, j, i, 0)),
        compiler_params=pltpu.CompilerParams(
            dimension_semantics=("parallel", "parallel", "arbitrary"), vmem_limit_bytes=VMEM_LIMIT),
        name="na_qkv",
    )(x3, w)


def _na_block_offsets(rows):
    nblk = rows // NA_QROWS
    idx = np.full((3, NA_QROWS, NA_KROWS), NA_MASKED, np.int32)
    for cls, rb in enumerate((0, 1, nblk - 1)):
        r0 = rb * NA_QROWS
        start = int(np.clip(r0 - NA_WIN_ROWS // 2, 0, rows - NA_KROWS))
        for ri in range(NA_QROWS):
            r = r0 + ri
            row_start = int(np.clip(r - NA_WIN_ROWS // 2, 0, rows - NA_WIN_ROWS))
            for ki in range(NA_KROWS):
                kr = start + ki
                if row_start <= kr < row_start + NA_WIN_ROWS:
                    idx[cls, ri, ki] = kr - r + (NA_WIN_ROWS - 1)
    return idx


def _na_attn_kernel(q_ref, k_ref, v_ref, t_ref, o_ref, va_ref, bias_ref, *, rows):
    hd = HEAD_DIM
    nq = NA_QROWS * GRID_W
    nkeys = NA_KROWS * GRID_W
    nblk = rows // NA_QROWS

    @pl.when(pl.program_id(1) == 0)
    def _():
        idx = _na_block_offsets(rows)
        for cls in range(3):
            for ri in range(NA_QROWS):
                for kp in range(NA_KROWS // 2):
                    d0, d1 = int(idx[cls, ri, 2 * kp]), int(idx[cls, ri, 2 * kp + 1])
                    if d0 == NA_MASKED and d1 == NA_MASKED:
                        tile = jnp.full((GRID_W, 2 * GRID_W), NEG_BIG, F32)
                    else:
                        tile = jnp.concatenate([t_ref[0, d0], t_ref[0, d1]], axis=1)
                    bias_ref[cls, ri * GRID_W:(ri + 1) * GRID_W,
                             kp * 2 * GRID_W:(kp + 1) * 2 * GRID_W] = tile

    va_ref[:, :hd] = v_ref[0, 0]
    va_ref[:, hd:] = jnp.ones((va_ref.shape[0], hd), BF16)

    def block_scores(rb):
        qoff = pl.multiple_of(rb * nq, nq)
        start = jnp.clip(rb * NA_QROWS - NA_WIN_ROWS // 2, 0, rows - NA_KROWS)
        koff = pl.multiple_of(start * GRID_W, GRID_W)
        cls = jnp.where(rb == 0, 0, jnp.where(rb == nblk - 1, 2, 1))
        q = q_ref[0, 0, pl.ds(qoff, nq), :]
        k = k_ref[0, 0, pl.ds(koff, nkeys), :]
        s = lax.dot_general(q, k, (((1,), (1,)), ((), ())), preferred_element_type=F32)
        return qoff, koff, s + bias_ref[cls]

    def body(it, carry):
        items = [block_scores(it * NA_UNROLL + u) for u in range(NA_UNROLL)]
        for qoff, koff, s in items:
            m = jnp.max(s, axis=1, keepdims=True)
            p = jnp.exp2(s - m).astype(BF16)
            oa = jnp.dot(p, va_ref[pl.ds(koff, nkeys), :], preferred_element_type=F32)
            o_ref[0, pl.ds(qoff, nq), :] = (oa[:, :hd] / oa[:, hd:hd + 1]).astype(BF16)
        return carry

    lax.fori_loop(0, nblk // NA_UNROLL, body, 0)


def _na_attn(qkv, t):
    bsz, _, s, hd = qkv.shape
    rows = s // GRID_W
    nq = NA_QROWS * GRID_W
    nkeys = NA_KROWS * GRID_W
    assert rows >= NA_KROWS and rows % (NA_QROWS * NA_UNROLL) == 0 and NA_KROWS % 2 == 0
    return pl.pallas_call(
        functools.partial(_na_attn_kernel, rows=rows),
        out_shape=jax.ShapeDtypeStruct((bsz, s, N_HEADS * hd), BF16),
        grid=(N_HEADS, bsz),
        in_specs=[
            pl.BlockSpec((1, 1, s, hd), lambda h, b: (b, h, 0, 0)),
            pl.BlockSpec((1, 1, s, hd), lambda h, b: (b, N_HEADS + h, 0, 0)),
            pl.BlockSpec((1, 1, s, hd), lambda h, b: (b, 2 * N_HEADS + h, 0, 0)),
            pl.BlockSpec((1, NA_MASKED + 1, GRID_W, GRID_W), lambda h, b: (h, 0, 0, 0)),
        ],
        out_specs=pl.BlockSpec((1, s, hd), lambda h, b: (b, 0, h)),
        scratch_shapes=[pltpu.VMEM((s, 2 * hd), BF16), pltpu.VMEM((3, nq, nkeys), F32)],
        compiler_params=pltpu.CompilerParams(
            dimension_semantics=("parallel", "arbitrary"), vmem_limit_bytes=VMEM_LIMIT),
        name="na_attn",
    )(qkv, qkv, qkv, t)


def _na_bias_tiles(rpb):
    h = rpb.shape[0]
    c = np.arange(GRID_W)
    col_start = np.clip(c - NA_WIN_COLS // 2, 0, GRID_W - NA_WIN_COLS)
    kc = np.arange(GRID_W)
    col_ok = (kc[None, :] >= col_start[:, None]) & (kc[None, :] < col_start[:, None] + NA_WIN_COLS)
    pad = GRID_W - NA_WIN_COLS
    padded = jnp.pad(rpb * LOG2E, ((0, 0), (0, 0), (pad, pad)))
    t = jnp.stack([padded[:, :, GRID_W - 1 - ci: 2 * GRID_W - 1 - ci] for ci in range(GRID_W)], axis=2)
    t = jnp.where(jnp.asarray(col_ok)[None, None], t, NEG_BIG)
    return jnp.concatenate([t, jnp.full((h, 1, GRID_W, GRID_W), NEG_BIG, F32)], axis=1)


def _rope_tables(seq):
    t = np.arange(seq)
    row = (t // GRID_W).astype(np.float32)
    col = (t % GRID_W).astype(np.float32)
    axis_dim = HEAD_DIM // 2
    inv_freq = np.float32(ROPE_THETA) ** (-np.arange(0, axis_dim, 2, dtype=np.float32) / np.float32(axis_dim))
    ar = row[:, None] * inv_freq
    ac = col[:, None] * inv_freq
    cos = np.concatenate([np.cos(ar), np.cos(ac), np.cos(ar), np.cos(ac)], axis=1)
    sin = np.concatenate([-np.sin(ar), -np.sin(ac), np.sin(ar), np.sin(ac)], axis=1)
    return jnp.asarray(cos, F32), jnp.asarray(sin, F32)


def _rope_layout(a, n_heads):
    nf = HEAD_DIM // 4
    lead = a.shape[:-1]
    n = n_heads * HEAD_DIM
    p = a[..., :n].reshape(*lead, n_heads, 2, 2, nf)
    p = jnp.swapaxes(p, -3, -2).reshape(*lead, n)
    return jnp.concatenate([p, a[..., n:]], axis=-1)


def _gain_rows(g):
    g = _rope_layout(g, 1)
    return jnp.stack([g, jnp.roll(g, HEAD_DIM // 2)])


def kernel(x, ln_g, ln_b, ffn_w_in, ffn_w_out, ga_w_qkv, ga_q_norm, ga_k_norm, ga_w_o,
           na_w_qkv, na_rpb, na_w_o):
    bsz, s, d = x.shape
    m = bsz * s
    cos, sin = _rope_tables(s)

    def ln_p(i, k):
        return ln_g[i, k][None, :], ln_b[i, k][None, :]

    w_in = ffn_w_in.astype(BF16)
    w_out = ffn_w_out.astype(BF16)

    def ffn(h, i, k):
        return _ffn_ln(h, w_in, w_out, *ln_p(i, 2 * k), i, k)

    h = x.reshape(m, d)
    for i in range(DEPTH):
        h = ffn(h, i, 0)
        j = i // 2
        if i % 2 == 0:
            qg, kg = _gain_rows(ga_q_norm[j]), _gain_rows(ga_k_norm[j])
            w_qkv = _rope_layout(ga_w_qkv[j], N_HEADS + GA_KV_HEADS).astype(BF16)
            q, k, vt = _gqa_qkv(h.reshape(bsz, s, d), w_qkv, cos, sin, qg, kg)
            o = _gqa_flash(q, k, vt, qg, kg)
            w_o = ga_w_o[j]
        else:
            qkv = _na_qkv(h.reshape(bsz, s, d), na_w_qkv[j].astype(BF16))
            o = _na_attn(qkv, _na_bias_tiles(na_rpb[j]))
            w_o = na_w_o[j]
        h = _proj_ln(o.reshape(m, d), h, w_o.astype(BF16), *ln_p(i, 1))
        h = ffn(h, i, 1)
    return h.reshape(bsz, s, d)
```

```python
import functools
import math

import jax
import jax.numpy as jnp
import numpy as np
from jax import lax
from jax.experimental import pallas as pl
from jax.experimental.pallas import tpu as pltpu

F32 = jnp.float32
BF16 = jnp.bfloat16

GRID_W = 64
HEAD_DIM = 128
N_HEADS = 16
GA_KV_HEADS = 4
GA_GROUPS = N_HEADS // GA_KV_HEADS
ROPE_THETA = 10000.0
NA_WIN_ROWS = 8
NA_WIN_COLS = 16
DEPTH = 2
DEEPNORM_ALPHA = (2 * DEPTH) ** 0.25
LN_EPS = 1e-5
QK_EPS = 1e-6
ATTN_SCALE = HEAD_DIM ** -0.5
LOG2E = math.log2(math.e)
SCORE_SCALE = ATTN_SCALE * LOG2E
NEG_BIG = -1e30
VT_ROWS = HEAD_DIM + 16
BOUND_SLACK = 1.0 + 2.0 ** -5
STABILISER_MAX = 48.0
FLASH_UNROLL = 16

VMEM_LIMIT = 56 * 1024 * 1024
VMEM_LIMIT_FFN = 62 * 1024 * 1024

NA_QROWS = 4
NA_KROWS = NA_QROWS + NA_WIN_ROWS
NA_MASKED = 2 * NA_WIN_ROWS - 1
NA_UNROLL = 32


def _layer_norm(z, g, b, z_scale=1.0):
    mu = jnp.mean(z, axis=-1, keepdims=True)
    zc = z - mu
    var = jnp.mean(zc * zc, axis=-1, keepdims=True)
    return zc * lax.rsqrt(var + (z_scale * z_scale) * LN_EPS) * g + b


def _ffn_ln_kernel(x_ref, wg_ref, wu_ref, wo_ref, g_ref, b_ref, o_ref, xb_ref, *, sub, sub_edge):
    j = pl.program_id(1)
    last = pl.num_programs(1) - 1
    tm = xb_ref.shape[0]

    def swiglu_out(xb):
        gate = jnp.dot(xb, wg_ref[...], preferred_element_type=F32)
        up = jnp.dot(xb, wu_ref[...], preferred_element_type=F32)
        h = (gate * jax.nn.sigmoid(gate) * up).astype(BF16)
        return jnp.dot(h, wo_ref[...], preferred_element_type=F32)

    @pl.when(j == 0)
    def _():
        for r in range(0, tm, sub):
            xb = x_ref[r:r + sub, :].astype(BF16)
            xb_ref[r:r + sub, :] = xb
            o_ref[r:r + sub, :] = swiglu_out(xb)

    @pl.when(jnp.logical_and(j > 0, j < last))
    def _():
        for r in range(0, tm, sub):
            o_ref[r:r + sub, :] += swiglu_out(xb_ref[r:r + sub, :])

    @pl.when(j == last)
    def _():
        def total(r):
            return o_ref[r:r + sub_edge, :] + swiglu_out(xb_ref[r:r + sub_edge, :])

        y_next = total(0)
        for r in range(0, tm, sub_edge):
            y = y_next
            if r + sub_edge < tm:
                y_next = total(r + sub_edge)
            z2 = (2.0 * DEEPNORM_ALPHA) * x_ref[r:r + sub_edge, :] + y
            o_ref[r:r + sub_edge, :] = _layer_norm(z2, g_ref[...], b_ref[...], z_scale=2.0)


def _ffn_ln(x2, w_in, w_out, g, b, layer, which, *, tm=1024, tf=512, sub=512, sub_edge=256):
    m, d = x2.shape
    f = w_out.shape[2]
    nf = f // tf
    assert m % tm == 0 and tm % sub == 0 and tm % sub_edge == 0 and f % tf == 0 and nf >= 2
    return pl.pallas_call(
        functools.partial(_ffn_ln_kernel, sub=sub, sub_edge=sub_edge),
        out_shape=jax.ShapeDtypeStruct((m, d), F32),
        grid=(m // tm, nf),
        in_specs=[
            pl.BlockSpec((tm, d), lambda i, j: (i, 0)),
            pl.BlockSpec((None, None, d, tf), lambda i, j: (layer, which, 0, j)),
            pl.BlockSpec((None, None, d, tf), lambda i, j: (layer, which, 0, j + nf)),
            pl.BlockSpec((None, None, tf, d), lambda i, j: (layer, which, j, 0)),
            pl.BlockSpec((1, d), lambda i, j: (0, 0)),
            pl.BlockSpec((1, d), lambda i, j: (0, 0)),
        ],
        out_specs=pl.BlockSpec((tm, d), lambda i, j: (i, 0)),
        scratch_shapes=[pltpu.VMEM((tm, d), BF16)],
        compiler_params=pltpu.CompilerParams(
            dimension_semantics=("parallel", "arbitrary"), vmem_limit_bytes=VMEM_LIMIT_FFN),
        name="ffn_ln",
    )(x2, w_in, w_in, w_out, g, b)


def _proj_ln_kernel(a_ref, x_ref, w_ref, g_ref, b_ref, o_ref, *, sub):
    for r in range(0, a_ref.shape[0], sub):
        y = jnp.dot(a_ref[r:r + sub, :], w_ref[...], preferred_element_type=F32)
        z = DEEPNORM_ALPHA * x_ref[r:r + sub, :] + y
        o_ref[r:r + sub, :] = _layer_norm(z, g_ref[...], b_ref[...])


def _proj_ln(a2, x2, w, g, b, *, tm=1024, sub=256):
    m, d = x2.shape
    k = a2.shape[1]
    return pl.pallas_call(
        functools.partial(_proj_ln_kernel, sub=sub),
        out_shape=jax.ShapeDtypeStruct((m, d), F32),
        grid=(m // tm,),
        in_specs=[
            pl.BlockSpec((tm, k), lambda i: (i, 0)),
            pl.BlockSpec((tm, d), lambda i: (i, 0)),
            pl.BlockSpec((k, d), lambda i: (0, 0)),
            pl.BlockSpec((1, d), lambda i: (0, 0)),
            pl.BlockSpec((1, d), lambda i: (0, 0)),
        ],
        out_specs=pl.BlockSpec((tm, d), lambda i: (i, 0)),
        compiler_params=pltpu.CompilerParams(
            dimension_semantics=("parallel",), vmem_limit_bytes=VMEM_LIMIT),
        name="proj_ln",
    )(a2, x2, w, g, b)


def _score_bound(qg, kg):
    qmax = jnp.max(jnp.abs(qg), axis=1, keepdims=True)
    kmax = jnp.max(jnp.abs(kg), axis=1, keepdims=True)
    return (SCORE_SCALE * HEAD_DIM * BOUND_SLACK) * qmax * kmax


def _gqa_qkv_kernel(x_ref, w_ref, cos_ref, sin_ref, qg_ref, kg_ref, q_ref, k_ref, vt_ref,
                    *, heads_per_dot):
    tm = x_ref.shape[1]
    xb = x_ref[0].astype(BF16)
    lane0 = lax.broadcasted_iota(jnp.int32, (tm, HEAD_DIM), 1) == 0
    cos = cos_ref[...]
    sin = sin_ref[...]
    q_cos = cos * (qg_ref[0:1, :] * SCORE_SCALE)
    q_sin = sin * (qg_ref[1:2, :] * SCORE_SCALE)
    k_cos = cos * kg_ref[0:1, :]
    k_sin = sin * kg_ref[1:2, :]

    def norm_rope(yh, c, s):
        ms = jnp.mean(yh * yh, axis=-1, keepdims=True)
        return (yh * c + pltpu.roll(yh, HEAD_DIM // 2, 1) * s) * lax.rsqrt(ms + QK_EPS)

    q_hi = jnp.where(lane0, -_score_bound(qg_ref[0:1, :], kg_ref[0:1, :]), 0.0).astype(BF16)
    k_hi = jnp.where(lane0, 1.0, 0.0).astype(BF16)
    ones = jnp.ones((VT_ROWS - HEAD_DIM, tm), BF16)

    n_all = N_HEADS + 2 * GA_KV_HEADS

    def project(h0):
        c0 = h0 * HEAD_DIM
        return jnp.dot(xb, w_ref[:, c0:c0 + heads_per_dot * HEAD_DIM], preferred_element_type=F32)

    y_next = project(0)
    for h0 in range(0, n_all, heads_per_dot):
        y = y_next
        if h0 + heads_per_dot < n_all:
            y_next = project(h0 + heads_per_dot)
        for t in range(heads_per_dot):
            h = h0 + t
            yh = y[:, t * HEAD_DIM:(t + 1) * HEAD_DIM]
            if h < N_HEADS:
                q_ref[0, h, :, :HEAD_DIM] = norm_rope(yh, q_cos, q_sin).astype(BF16)
                q_ref[0, h, :, HEAD_DIM:] = q_hi
            elif h < N_HEADS + GA_KV_HEADS:
                k_ref[0, h - N_HEADS, :, :HEAD_DIM] = norm_rope(yh, k_cos, k_sin).astype(BF16)
                k_ref[0, h - N_HEADS, :, HEAD_DIM:] = k_hi
            else:
                hv = h - N_HEADS - GA_KV_HEADS
                vt_ref[0, hv, 0, :HEAD_DIM, :] = yh.T.astype(BF16)
                vt_ref[0, hv, 0, HEAD_DIM:, :] = ones


def _gqa_qkv(x3, w, cos, sin, qg, kg, *, tm=512, heads_per_dot=4):
    bsz, s, d = x3.shape
    n = w.shape[1]
    hd = HEAD_DIM
    return pl.pallas_call(
        functools.partial(_gqa_qkv_kernel, heads_per_dot=heads_per_dot),
        out_shape=(
            jax.ShapeDtypeStruct((bsz, N_HEADS, s, 2 * hd), BF16),
            jax.ShapeDtypeStruct((bsz, GA_KV_HEADS, s, 2 * hd), BF16),
            jax.ShapeDtypeStruct((bsz, GA_KV_HEADS, s // tm, VT_ROWS, tm), BF16),
        ),
        grid=(bsz, s // tm),
        in_specs=[
            pl.BlockSpec((1, tm, d), lambda b, i: (b, i, 0)),
            pl.BlockSpec((d, n), lambda b, i: (0, 0)),
            pl.BlockSpec((tm, hd), lambda b, i: (i, 0)),
            pl.BlockSpec((tm, hd), lambda b, i: (i, 0)),
            pl.BlockSpec((2, hd), lambda b, i: (0, 0)),
            pl.BlockSpec((2, hd), lambda b, i: (0, 0)),
        ],
        out_specs=(
            pl.BlockSpec((1, N_HEADS, tm, 2 * hd), lambda b, i: (b, 0, i, 0)),
            pl.BlockSpec((1, GA_KV_HEADS, tm, 2 * hd), lambda b, i: (b, 0, i, 0)),
            pl.BlockSpec((1, GA_KV_HEADS, 1, VT_ROWS, tm), lambda b, i: (b, 0, i, 0, 0)),
        ),
        compiler_params=pltpu.CompilerParams(
            dimension_semantics=("parallel", "parallel"), vmem_limit_bytes=VMEM_LIMIT),
        name="gqa_qkv",
    )(x3, w, cos, sin, qg, kg)


def _gqa_flash_kernel(q_ref, k_ref, vt_ref, qg_ref, kg_ref, o_ref, qt_ref, mx_ref, acc_ref, st_ref,
                      *, tq, tk):
    hd = HEAD_DIM
    nk = k_ref.shape[2] // tk
    unroll = math.gcd(nk, FLASH_UNROLL)
    exact_max = jnp.max(_score_bound(qg_ref[0:1, :], kg_ref[0:1, :])) > STABILISER_MAX
    mx_ref[...] = jnp.zeros(mx_ref.shape, BF16)

    @pl.when(exact_max)
    def _():
        row0 = lax.broadcasted_iota(jnp.int32, (16, tq), 0) == 0
        for h in range(GA_GROUPS):
            def max_body(i, m):
                off = pl.multiple_of(i * tk, tk)
                s = lax.dot_general(k_ref[0, 0, pl.ds(off, tk), :hd], q_ref[0, h, :, :hd],
                                    (((1,), (1,)), ((), ())), preferred_element_type=F32)
                return jnp.maximum(m, jnp.max(s, axis=0, keepdims=True))

            m = lax.fori_loop(0, nk, max_body, jnp.full((1, tq), NEG_BIG, F32))
            mx_ref[h] = jnp.where(row0, -m, 0.0).astype(BF16)

    for h in range(GA_GROUPS):
        qt = q_ref[0, h].T
        qt_ref[h] = qt
        qt_ref[h, hd:hd + 16, :] = jnp.where(exact_max, mx_ref[h], qt[hd:hd + 16, :])
    acc_ref[...] = jnp.zeros(acc_ref.shape, F32)

    def scores(i, h):
        off = pl.multiple_of(i * tk, tk)
        return jnp.dot(k_ref[0, 0, pl.ds(off, tk), :], qt_ref[h],
                       preferred_element_type=F32)

    st_ref[...] = scores(0, 0)

    def body(it, carry):
        st = st_ref[...]
        for u in range(unroll):
            i = it * unroll + u
            vt = vt_ref[0, 0, i]
            for h in range(GA_GROUPS):
                if h + 1 < GA_GROUPS:
                    st_next = scores(i, h + 1)
                else:
                    st_next = scores(jnp.minimum(i + 1, nk - 1), 0)
                p = jnp.exp2(st).astype(BF16)
                acc_ref[h] += jnp.dot(vt, p, preferred_element_type=F32)
                st = st_next
        st_ref[...] = st
        return carry

    lax.fori_loop(0, nk // unroll, body, 0)
    for h in range(GA_GROUPS):
        out_t = acc_ref[h, :hd, :] / acc_ref[h, hd:hd + 1, :]
        o_ref[0, :, h * hd:(h + 1) * hd] = out_t.T.astype(BF16)


def _gqa_flash(q, k, vt, qg, kg, *, tq=512):
    bsz, _, s, hd2 = q.shape
    hd = hd2 // 2
    nk, tk = vt.shape[2], vt.shape[4]
    assert s % tq == 0
    return pl.pallas_call(
        functools.partial(_gqa_flash_kernel, tq=tq, tk=tk),
        out_shape=jax.ShapeDtypeStruct((bsz, s, N_HEADS * hd), BF16),
        grid=(bsz, GA_KV_HEADS, s // tq),
        in_specs=[
            pl.BlockSpec((1, GA_GROUPS, tq, hd2), lambda b, g, i: (b, g, i, 0)),
            pl.BlockSpec((1, 1, s, hd2), lambda b, g, i: (b, g, 0, 0)),
            pl.BlockSpec((1, 1, nk, VT_ROWS, tk), lambda b, g, i: (b, g, 0, 0, 0)),
            pl.BlockSpec((2, hd), lambda b, g, i: (0, 0)),
            pl.BlockSpec((2, hd), lambda b, g, i: (0, 0)),
        ],
        out_specs=pl.BlockSpec((1, tq, GA_GROUPS * hd), lambda b, g, i: (b, i, g)),
        scratch_shapes=[pltpu.VMEM((GA_GROUPS, hd2, tq), BF16), pltpu.VMEM((GA_GROUPS, 16, tq), BF16),
                        pltpu.VMEM((GA_GROUPS, VT_ROWS, tq), F32), pltpu.VMEM((tk, tq), F32)],
        compiler_params=pltpu.CompilerParams(
            dimension_semantics=("parallel", "parallel", "parallel"), vmem_limit_bytes=VMEM_LIMIT),
        name="gqa_flash",
    )(q, k, vt, qg, kg)


def _na_qkv_kernel(x_ref, w_ref, o_ref, *, q_tiles):
    xb = x_ref[0].astype(BF16)
    y = jnp.dot(xb, w_ref[...], preferred_element_type=F32)
    y = y * jnp.where(pl.program_id(2) < q_tiles, SCORE_SCALE, 1.0)
    for t in range(o_ref.shape[1]):
        o_ref[0, t] = y[:, t * HEAD_DIM:(t + 1) * HEAD_DIM].astype(BF16)


def _na_qkv(x3, w, *, tm=1024, tn=2048):
    bsz, s, d = x3.shape
    n = w.shape[1]
    hd = HEAD_DIM
    return pl.pallas_call(
        functools.partial(_na_qkv_kernel, q_tiles=N_HEADS * hd // tn),
        out_shape=jax.ShapeDtypeStruct((bsz, n // hd, s, hd), BF16),
        grid=(bsz, s // tm, n // tn),
        in_specs=[
            pl.BlockSpec((1, tm, d), lambda b, i, j: (b, i, 0)),
            pl.BlockSpec((d, tn), lambda b, i, j: (0, j)),
        ],
        out_specs=pl.BlockSpec((1, tn // hd, tm, hd), lambda b, i, j: (b, j, i, 0)),
        compiler_params=pltpu.CompilerParams(
            dimension_semantics=("parallel", "parallel", "arbitrary"), vmem_limit_bytes=VMEM_LIMIT),
        name="na_qkv",
    )(x3, w)


def _na_block_offsets(rows):
    nblk = rows // NA_QROWS
    idx = np.full((3, NA_QROWS, NA_KROWS), NA_MASKED, np.int32)
    for cls, rb in enumerate((0, 1, nblk - 1)):
        r0 = rb * NA_QROWS
        start = int(np.clip(r0 - NA_WIN_ROWS // 2, 0, rows - NA_KROWS))
        for ri in range(NA_QROWS):
            r = r0 + ri
            row_start = int(np.clip(r - NA_WIN_ROWS // 2, 0, rows - NA_WIN_ROWS))
            for ki in range(NA_KROWS):
                kr = start + ki
                if row_start <= kr < row_start + NA_WIN_ROWS:
                    idx[cls, ri, ki] = kr - r + (NA_WIN_ROWS - 1)
    return idx


def _na_attn_kernel(q_ref, k_ref, v_ref, t_ref, o_ref, va_ref, bias_ref, *, rows):
    hd = HEAD_DIM
    nq = NA_QROWS * GRID_W
    nkeys = NA_KROWS * GRID_W
    nblk = rows // NA_QROWS
    unroll = math.gcd(nblk, NA_UNROLL)

    @pl.when(pl.program_id(1) == 0)
    def _():
        idx = _na_block_offsets(rows)
        for cls in range(3):
            for ri in range(NA_QROWS):
                for kp in range(NA_KROWS // 2):
                    d0, d1 = int(idx[cls, ri, 2 * kp]), int(idx[cls, ri, 2 * kp + 1])
                    if d0 == NA_MASKED and d1 == NA_MASKED:
                        tile = jnp.full((GRID_W, 2 * GRID_W), NEG_BIG, F32)
                    else:
                        tile = jnp.concatenate([t_ref[0, d0], t_ref[0, d1]], axis=1)
                    bias_ref[cls, ri * GRID_W:(ri + 1) * GRID_W,
                             kp * 2 * GRID_W:(kp + 1) * 2 * GRID_W] = tile

    va_ref[:, :hd] = v_ref[0, 0]
    va_ref[:, hd:] = jnp.ones((va_ref.shape[0], hd), BF16)

    def block_scores(rb):
        qoff = pl.multiple_of(rb * nq, nq)
        start = jnp.clip(rb * NA_QROWS - NA_WIN_ROWS // 2, 0, rows - NA_KROWS)
        koff = pl.multiple_of(start * GRID_W, GRID_W)
        cls = jnp.where(rb == 0, 0, jnp.where(rb == nblk - 1, 2, 1))
        q = q_ref[0, 0, pl.ds(qoff, nq), :]
        k = k_ref[0, 0, pl.ds(koff, nkeys), :]
        s = lax.dot_general(q, k, (((1,), (1,)), ((), ())), preferred_element_type=F32)
        return qoff, koff, s + bias_ref[cls]

    def body(it, carry):
        items = [block_scores(it * unroll + u) for u in range(unroll)]
        for qoff, koff, s in items:
            m = jnp.max(s, axis=1, keepdims=True)
            p = jnp.exp2(s - m).astype(BF16)
            oa = jnp.dot(p, va_ref[pl.ds(koff, nkeys), :], preferred_element_type=F32)
            o_ref[0, pl.ds(qoff, nq), :] = (oa[:, :hd] / oa[:, hd:hd + 1]).astype(BF16)
        return carry

    lax.fori_loop(0, nblk // unroll, body, 0)


def _na_attn(qkv, t):
    bsz, _, s, hd = qkv.shape
    rows = s // GRID_W
    nq = NA_QROWS * GRID_W
    nkeys = NA_KROWS * GRID_W
    assert rows >= NA_KROWS and rows % NA_QROWS == 0 and NA_KROWS % 2 == 0
    return pl.pallas_call(
        functools.partial(_na_attn_kernel, rows=rows),
        out_shape=jax.ShapeDtypeStruct((bsz, s, N_HEADS * hd), BF16),
        grid=(N_HEADS, bsz),
        in_specs=[
            pl.BlockSpec((1, 1, s, hd), lambda h, b: (b, h, 0, 0)),
            pl.BlockSpec((1, 1, s, hd), lambda h, b: (b, N_HEADS + h, 0, 0)),
            pl.BlockSpec((1, 1, s, hd), lambda h, b: (b, 2 * N_HEADS + h, 0, 0)),
            pl.BlockSpec((1, NA_MASKED + 1, GRID_W, GRID_W), lambda h, b: (h, 0, 0, 0)),
        ],
        out_specs=pl.BlockSpec((1, s, hd), lambda h, b: (b, 0, h)),
        scratch_shapes=[pltpu.VMEM((s, 2 * hd), BF16), pltpu.VMEM((3, nq, nkeys), F32)],
        compiler_params=pltpu.CompilerParams(
            dimension_semantics=("parallel", "arbitrary"), vmem_limit_bytes=VMEM_LIMIT),
        name="na_attn",
    )(qkv, qkv, qkv, t)


def _na_bias_tiles(rpb):
    h = rpb.shape[0]
    c = np.arange(GRID_W)
    col_start = np.clip(c - NA_WIN_COLS // 2, 0, GRID_W - NA_WIN_COLS)
    kc = np.arange(GRID_W)
    col_ok = (kc[None, :] >= col_start[:, None]) & (kc[None, :] < col_start[:, None] + NA_WIN_COLS)
    pad = GRID_W - NA_WIN_COLS
    padded = jnp.pad(rpb * LOG2E, ((0, 0), (0, 0), (pad, pad)))
    t = jnp.stack([padded[:, :, GRID_W - 1 - ci: 2 * GRID_W - 1 - ci] for ci in range(GRID_W)], axis=2)
    t = jnp.where(jnp.asarray(col_ok)[None, None], t, NEG_BIG)
    return jnp.concatenate([t, jnp.full((h, 1, GRID_W, GRID_W), NEG_BIG, F32)], axis=1)


def _rope_tables(seq):
    t = np.arange(seq)
    row = (t // GRID_W).astype(np.float32)
    col = (t % GRID_W).astype(np.float32)
    axis_dim = HEAD_DIM // 2
    inv_freq = np.float32(ROPE_THETA) ** (-np.arange(0, axis_dim, 2, dtype=np.float32) / np.float32(axis_dim))
    ar = row[:, None] * inv_freq
    ac = col[:, None] * inv_freq
    cos = np.concatenate([np.cos(ar), np.cos(ac), np.cos(ar), np.cos(ac)], axis=1)
    sin = np.concatenate([-np.sin(ar), -np.sin(ac), np.sin(ar), np.sin(ac)], axis=1)
    return jnp.asarray(cos, F32), jnp.asarray(sin, F32)


def _rope_layout(a, n_heads):
    nf = HEAD_DIM // 4
    lead = a.shape[:-1]
    n = n_heads * HEAD_DIM
    p = a[..., :n].reshape(*lead, n_heads, 2, 2, nf)
    p = jnp.swapaxes(p, -3, -2).reshape(*lead, n)
    return jnp.concatenate([p, a[..., n:]], axis=-1)


def _gain_rows(g):
    g = _rope_layout(g, 1)
    return jnp.stack([g, jnp.roll(g, HEAD_DIM // 2)])


def kernel(x, ln_g, ln_b, ffn_w_in, ffn_w_out, ga_w_qkv, ga_q_norm, ga_k_norm, ga_w_o,
           na_w_qkv, na_rpb, na_w_o):
    bsz, s, d = x.shape
    m = bsz * s
    cos, sin = _rope_tables(s)

    def ln_p(i, k):
        return ln_g[i, k][None, :], ln_b[i, k][None, :]

    w_in = ffn_w_in.astype(BF16)
    w_out = ffn_w_out.astype(BF16)

    def ffn(h, i, k):
        return _ffn_ln(h, w_in, w_out, *ln_p(i, 2 * k), i, k)

    h = x.reshape(m, d)
    for i in range(DEPTH):
        h = ffn(h, i, 0)
        j = i // 2
        if i % 2 == 0:
            qg, kg = _gain_rows(ga_q_norm[j]), _gain_rows(ga_k_norm[j])
            w_qkv = _rope_layout(ga_w_qkv[j], N_HEADS + GA_KV_HEADS).astype(BF16)
            q, k, vt = _gqa_qkv(h.reshape(bsz, s, d), w_qkv, cos, sin, qg, kg)
            o = _gqa_flash(q, k, vt, qg, kg)
            w_o = ga_w_o[j]
        else:
            qkv = _na_qkv(h.reshape(bsz, s, d), na_w_qkv[j].astype(BF16))
            o = _na_attn(qkv, _na_bias_tiles(na_rpb[j]))
            w_o = na_w_o[j]
        h = _proj_ln(o.reshape(m, d), h, w_o.astype(BF16), *ln_p(i, 1))
        h = ffn(h, i, 1)
    return h.reshape(bsz, s, d)
```

```python
import functools
import math

import jax
import jax.numpy as jnp
import numpy as np
from jax import lax
from jax.experimental import pallas as pl
from jax.experimental.pallas import tpu as pltpu

F32 = jnp.float32
BF16 = jnp.bfloat16

GRID_W = 64
HEAD_DIM = 128
N_HEADS = 16
GA_KV_HEADS = 4
GA_GROUPS = N_HEADS // GA_KV_HEADS
ROPE_THETA = 10000.0
NA_WIN_ROWS = 8
NA_WIN_COLS = 16
DEPTH = 2
DEEPNORM_ALPHA = (2 * DEPTH) ** 0.25
LN_EPS = 1e-5
QK_EPS = 1e-6
ATTN_SCALE = HEAD_DIM ** -0.5
LOG2E = math.log2(math.e)
SCORE_SCALE = ATTN_SCALE * LOG2E
NEG_BIG = -1e30
VT_ROWS = HEAD_DIM + 16
BOUND_SLACK = 1.0 + 2.0 ** -5
STABILISER_MAX = 48.0
FLASH_UNROLL = 16

VMEM_LIMIT = 56 * 1024 * 1024
VMEM_LIMIT_FFN = 62 * 1024 * 1024

NA_QROWS = 4
NA_KROWS = NA_QROWS + NA_WIN_ROWS
NA_MASKED = 2 * NA_WIN_ROWS - 1
NA_UNROLL = 32


def _layer_norm(z, g, b, z_scale=1.0):
    mu = jnp.mean(z, axis=-1, keepdims=True)
    zc = z - mu
    var = jnp.mean(zc * zc, axis=-1, keepdims=True)
    return zc * lax.rsqrt(var + (z_scale * z_scale) * LN_EPS) * g + b


def _ffn_ln_kernel(x_ref, wg_ref, wu_ref, wo_ref, g_ref, b_ref, o_ref, xb_ref, *, sub, sub_edge):
    j = pl.program_id(1)
    last = pl.num_programs(1) - 1
    tm = xb_ref.shape[0]

    def swiglu_out(xb):
        gate = jnp.dot(xb, wg_ref[...], preferred_element_type=F32)
        up = jnp.dot(xb, wu_ref[...], preferred_element_type=F32)
        h = (gate * jax.nn.sigmoid(gate) * up).astype(BF16)
        return jnp.dot(h, wo_ref[...], preferred_element_type=F32)

    @pl.when(j == 0)
    def _():
        for r in range(0, tm, sub):
            xb = x_ref[r:r + sub, :].astype(BF16)
            xb_ref[r:r + sub, :] = xb
            o_ref[r:r + sub, :] = swiglu_out(xb)

    @pl.when(jnp.logical_and(j > 0, j < last))
    def _():
        for r in range(0, tm, sub):
            o_ref[r:r + sub, :] += swiglu_out(xb_ref[r:r + sub, :])

    @pl.when(j == last)
    def _():
        def total(r):
            return o_ref[r:r + sub_edge, :] + swiglu_out(xb_ref[r:r + sub_edge, :])

        y_next = total(0)
        for r in range(0, tm, sub_edge):
            y = y_next
            if r + sub_edge < tm:
                y_next = total(r + sub_edge)
            z2 = (2.0 * DEEPNORM_ALPHA) * x_ref[r:r + sub_edge, :] + y
            o_ref[r:r + sub_edge, :] = _layer_norm(z2, g_ref[...], b_ref[...], z_scale=2.0)


def _ffn_ln(x2, w_in, w_out, g, b, layer, which, *, tm=1024, tf=512, sub=512, sub_edge=256):
    m, d = x2.shape
    f = w_out.shape[2]
    nf = f // tf
    assert m % tm == 0 and tm % sub == 0 and tm % sub_edge == 0 and f % tf == 0 and nf >= 2
    return pl.pallas_call(
        functools.partial(_ffn_ln_kernel, sub=sub, sub_edge=sub_edge),
        out_shape=jax.ShapeDtypeStruct((m, d), F32),
        grid=(m // tm, nf),
        in_specs=[
            pl.BlockSpec((tm, d), lambda i, j: (i, 0)),
            pl.BlockSpec((None, None, d, tf), lambda i, j: (layer, which, 0, j)),
            pl.BlockSpec((None, None, d, tf), lambda i, j: (layer, which, 0, j + nf)),
            pl.BlockSpec((None, None, tf, d), lambda i, j: (layer, which, j, 0)),
            pl.BlockSpec((1, d), lambda i, j: (0, 0)),
            pl.BlockSpec((1, d), lambda i, j: (0, 0)),
        ],
        out_specs=pl.BlockSpec((tm, d), lambda i, j: (i, 0)),
        scratch_shapes=[pltpu.VMEM((tm, d), BF16)],
        compiler_params=pltpu.CompilerParams(
            dimension_semantics=("parallel", "arbitrary"), vmem_limit_bytes=VMEM_LIMIT_FFN),
        name="ffn_ln",
    )(x2, w_in, w_in, w_out, g, b)


def _proj_ln_kernel(a_ref, x_ref, w_ref, g_ref, b_ref, o_ref, *, sub):
    for r in range(0, a_ref.shape[0], sub):
        y = jnp.dot(a_ref[r:r + sub, :], w_ref[...], preferred_element_type=F32)
        z = DEEPNORM_ALPHA * x_ref[r:r + sub, :] + y
        o_ref[r:r + sub, :] = _layer_norm(z, g_ref[...], b_ref[...])


def _proj_ln(a2, x2, w, g, b, *, tm=1024, sub=256):
    m, d = x2.shape
    k = a2.shape[1]
    return pl.pallas_call(
        functools.partial(_proj_ln_kernel, sub=sub),
        out_shape=jax.ShapeDtypeStruct((m, d), F32),
        grid=(m // tm,),
        in_specs=[
            pl.BlockSpec((tm, k), lambda i: (i, 0)),
            pl.BlockSpec((tm, d), lambda i: (i, 0)),
            pl.BlockSpec((k, d), lambda i: (0, 0)),
            pl.BlockSpec((1, d), lambda i: (0, 0)),
            pl.BlockSpec((1, d), lambda i: (0, 0)),
        ],
        out_specs=pl.BlockSpec((tm, d), lambda i: (i, 0)),
        compiler_params=pltpu.CompilerParams(
            dimension_semantics=("parallel",), vmem_limit_bytes=VMEM_LIMIT),
        name="proj_ln",
    )(a2, x2, w, g, b)


def _score_bound(qg, kg):
    qmax = jnp.max(jnp.abs(qg), axis=1, keepdims=True)
    kmax = jnp.max(jnp.abs(kg), axis=1, keepdims=True)
    return (SCORE_SCALE * HEAD_DIM * BOUND_SLACK) * qmax * kmax


def _gqa_qkv_kernel(x_ref, w_ref, cos_ref, sin_ref, qg_ref, kg_ref, q_ref, k_ref, vt_ref,
                    *, heads_per_dot):
    tm = x_ref.shape[1]
    xb = x_ref[0].astype(BF16)
    lane0 = lax.broadcasted_iota(jnp.int32, (tm, HEAD_DIM), 1) == 0
    cos = cos_ref[...]
    sin = sin_ref[...]
    q_cos = cos * (qg_ref[0:1, :] * SCORE_SCALE)
    q_sin = sin * (qg_ref[1:2, :] * SCORE_SCALE)
    k_cos = cos * kg_ref[0:1, :]
    k_sin = sin * kg_ref[1:2, :]

    def norm_rope(yh, c, s):
        ms = jnp.mean(yh * yh, axis=-1, keepdims=True)
        return (yh * c + pltpu.roll(yh, HEAD_DIM // 2, 1) * s) * lax.rsqrt(ms + QK_EPS)

    q_hi = jnp.where(lane0, -_score_bound(qg_ref[0:1, :], kg_ref[0:1, :]), 0.0).astype(BF16)
    k_hi = jnp.where(lane0, 1.0, 0.0).astype(BF16)
    ones = jnp.ones((VT_ROWS - HEAD_DIM, tm), BF16)

    n_all = N_HEADS + 2 * GA_KV_HEADS

    def project(h0):
        c0 = h0 * HEAD_DIM
        return jnp.dot(xb, w_ref[:, c0:c0 + heads_per_dot * HEAD_DIM], preferred_element_type=F32)

    y_next = project(0)
    for h0 in range(0, n_all, heads_per_dot):
        y = y_next
        if h0 + heads_per_dot < n_all:
            y_next = project(h0 + heads_per_dot)
        for t in range(heads_per_dot):
            h = h0 + t
            yh = y[:, t * HEAD_DIM:(t + 1) * HEAD_DIM]
            if h < N_HEADS:
                q_ref[0, h, :, :HEAD_DIM] = norm_rope(yh, q_cos, q_sin).astype(BF16)
                q_ref[0, h, :, HEAD_DIM:] = q_hi
            elif h < N_HEADS + GA_KV_HEADS:
                k_ref[0, h - N_HEADS, :, :HEAD_DIM] = norm_rope(yh, k_cos, k_sin).astype(BF16)
                k_ref[0, h - N_HEADS, :, HEAD_DIM:] = k_hi
            else:
                hv = h - N_HEADS - GA_KV_HEADS
                vt_ref[0, hv, 0, :HEAD_DIM, :] = yh.T.astype(BF16)
                vt_ref[0, hv, 0, HEAD_DIM:, :] = ones


def _gqa_qkv(x3, w, cos, sin, qg, kg, *, tm=512, heads_per_dot=4):
    bsz, s, d = x3.shape
    n = w.shape[1]
    hd = HEAD_DIM
    return pl.pallas_call(
        functools.partial(_gqa_qkv_kernel, heads_per_dot=heads_per_dot),
        out_shape=(
            jax.ShapeDtypeStruct((bsz, N_HEADS, s, 2 * hd), BF16),
            jax.ShapeDtypeStruct((bsz, GA_KV_HEADS, s, 2 * hd), BF16),
            jax.ShapeDtypeStruct((bsz, GA_KV_HEADS, s // tm, VT_ROWS, tm), BF16),
        ),
        grid=(bsz, s // tm),
        in_specs=[
            pl.BlockSpec((1, tm, d), lambda b, i: (b, i, 0)),
            pl.BlockSpec((d, n), lambda b, i: (0, 0)),
            pl.BlockSpec((tm, hd), lambda b, i: (i, 0)),
            pl.BlockSpec((tm, hd), lambda b, i: (i, 0)),
            pl.BlockSpec((2, hd), lambda b, i: (0, 0)),
            pl.BlockSpec((2, hd), lambda b, i: (0, 0)),
        ],
        out_specs=(
            pl.BlockSpec((1, N_HEADS, tm, 2 * hd), lambda b, i: (b, 0, i, 0)),
            pl.BlockSpec((1, GA_KV_HEADS, tm, 2 * hd), lambda b, i: (b, 0, i, 0)),
            pl.BlockSpec((1, GA_KV_HEADS, 1, VT_ROWS, tm), lambda b, i: (b, 0, i, 0, 0)),
        ),
        compiler_params=pltpu.CompilerParams(
            dimension_semantics=("parallel", "parallel"), vmem_limit_bytes=VMEM_LIMIT),
        name="gqa_qkv",
    )(x3, w, cos, sin, qg, kg)


def _gqa_flash_kernel(q_ref, k_ref, vt_ref, qg_ref, kg_ref, o_ref, qt_ref, mx_ref, acc_ref, st_ref,
                      *, tq, tk):
    hd = HEAD_DIM
    nk = k_ref.shape[2] // tk
    unroll = math.gcd(nk, FLASH_UNROLL)
    tiles = q_ref.shape[2] // tq
    exact_max = jnp.max(_score_bound(qg_ref[0:1, :], kg_ref[0:1, :])) > STABILISER_MAX
    mx_ref[...] = jnp.zeros(mx_ref.shape, BF16)

    @pl.when(exact_max)
    def _():
        row0 = lax.broadcasted_iota(jnp.int32, (16, tq), 0) == 0
        for t in range(tiles):
            for h in range(GA_GROUPS):
                def max_body(i, m):
                    off = pl.multiple_of(i * tk, tk)
                    s = lax.dot_general(k_ref[0, 0, pl.ds(off, tk), :hd],
                                        q_ref[0, h, t * tq:(t + 1) * tq, :hd],
                                        (((1,), (1,)), ((), ())), preferred_element_type=F32)
                    return jnp.maximum(m, jnp.max(s, axis=0, keepdims=True))

                m = lax.fori_loop(0, nk, max_body, jnp.full((1, tq), NEG_BIG, F32))
                mx_ref[t, h] = jnp.where(row0, -m, 0.0).astype(BF16)

    for t in range(tiles):
        for h in range(GA_GROUPS):
            qt = q_ref[0, h, t * tq:(t + 1) * tq, :].T
            qt_ref[t, h] = qt
            qt_ref[t, h, hd:hd + 16, :] = jnp.where(exact_max, mx_ref[t, h], qt[hd:hd + 16, :])
        acc_ref[t] = jnp.zeros(acc_ref.shape[1:], F32)

        def scores(i, h, t=t):
            off = pl.multiple_of(i * tk, tk)
            return jnp.dot(k_ref[0, 0, pl.ds(off, tk), :], qt_ref[t, h],
                           preferred_element_type=F32)

        st_ref[t] = scores(0, 0)

        def body(it, carry, t=t, scores=scores):
            st = st_ref[t]
            for u in range(unroll):
                i = it * unroll + u
                vt = vt_ref[0, 0, i]
                for h in range(GA_GROUPS):
                    if h + 1 < GA_GROUPS:
                        st_next = scores(i, h + 1)
                    else:
                        st_next = scores(jnp.minimum(i + 1, nk - 1), 0)
                    p = jnp.exp2(st).astype(BF16)
                    acc_ref[t, h] += jnp.dot(vt, p, preferred_element_type=F32)
                    st = st_next
            st_ref[t] = st
            return carry

        if unroll == nk:
            body(0, 0)
        else:
            lax.fori_loop(0, nk // unroll, body, 0)
        for h in range(GA_GROUPS):
            out_t = acc_ref[t, h, :hd, :] / acc_ref[t, h, hd:hd + 1, :]
            o_ref[0, t * tq:(t + 1) * tq, h * hd:(h + 1) * hd] = out_t.T.astype(BF16)


def _gqa_flash(q, k, vt, qg, kg, *, tq=512, tiles=2):
    bsz, _, s, hd2 = q.shape
    hd = hd2 // 2
    nk, tk = vt.shape[2], vt.shape[4]
    tb = tq * tiles
    assert s % tb == 0
    return pl.pallas_call(
        functools.partial(_gqa_flash_kernel, tq=tq, tk=tk),
        out_shape=jax.ShapeDtypeStruct((bsz, s, N_HEADS * hd), BF16),
        grid=(bsz, GA_KV_HEADS, s // tb),
        in_specs=[
            pl.BlockSpec((1, GA_GROUPS, tb, hd2), lambda b, g, i: (b, g, i, 0)),
            pl.BlockSpec((1, 1, s, hd2), lambda b, g, i: (b, g, 0, 0)),
            pl.BlockSpec((1, 1, nk, VT_ROWS, tk), lambda b, g, i: (b, g, 0, 0, 0)),
            pl.BlockSpec((2, hd), lambda b, g, i: (0, 0)),
            pl.BlockSpec((2, hd), lambda b, g, i: (0, 0)),
        ],
        out_specs=pl.BlockSpec((1, tb, GA_GROUPS * hd), lambda b, g, i: (b, i, g)),
        scratch_shapes=[pltpu.VMEM((tiles, GA_GROUPS, hd2, tq), BF16),
                        pltpu.VMEM((tiles, GA_GROUPS, 16, tq), BF16),
                        pltpu.VMEM((tiles, GA_GROUPS, VT_ROWS, tq), F32),
                        pltpu.VMEM((tiles, tk, tq), F32)],
        compiler_params=pltpu.CompilerParams(
            dimension_semantics=("parallel", "parallel", "parallel"), vmem_limit_bytes=VMEM_LIMIT),
        name="gqa_flash",
    )(q, k, vt, qg, kg)


def _na_qkv_kernel(x_ref, w_ref, o_ref, *, q_tiles):
    xb = x_ref[0].astype(BF16)
    y = jnp.dot(xb, w_ref[...], preferred_element_type=F32)
    y = y * jnp.where(pl.program_id(2) < q_tiles, SCORE_SCALE, 1.0)
    for t in range(o_ref.shape[1]):
        o_ref[0, t] = y[:, t * HEAD_DIM:(t + 1) * HEAD_DIM].astype(BF16)


def _na_qkv(x3, w, *, tm=1024, tn=2048):
    bsz, s, d = x3.shape
    n = w.shape[1]
    hd = HEAD_DIM
    return pl.pallas_call(
        functools.partial(_na_qkv_kernel, q_tiles=N_HEADS * hd // tn),
        out_shape=jax.ShapeDtypeStruct((bsz, n // hd, s, hd), BF16),
        grid=(bsz, s // tm, n // tn),
        in_specs=[
            pl.BlockSpec((1, tm, d), lambda b, i, j: (b, i, 0)),
            pl.BlockSpec((d, tn), lambda b, i, j: (0, j)),
        ],
        out_specs=pl.BlockSpec((1, tn // hd, tm, hd), lambda b, i, j: (b, j, i, 0)),
        compiler_params=pltpu.CompilerParams(
            dimension_semantics=("parallel", "parallel", "arbitrary"), vmem_limit_bytes=VMEM_LIMIT),
        name="na_qkv",
    )(x3, w)


def _na_block_offsets(rows):
    nblk = rows // NA_QROWS
    idx = np.full((3, NA_QROWS, NA_KROWS), NA_MASKED, np.int32)
    for cls, rb in enumerate((0, 1, nblk - 1)):
        r0 = rb * NA_QROWS
        start = int(np.clip(r0 - NA_WIN_ROWS // 2, 0, rows - NA_KROWS))
        for ri in range(NA_QROWS):
            r = r0 + ri
            row_start = int(np.clip(r - NA_WIN_ROWS // 2, 0, rows - NA_WIN_ROWS))
            for ki in range(NA_KROWS):
                kr = start + ki
                if row_start <= kr < row_start + NA_WIN_ROWS:
                    idx[cls, ri, ki] = kr - r + (NA_WIN_ROWS - 1)
    return idx


def _na_attn_kernel(q_ref, k_ref, v_ref, t_ref, o_ref, va_ref, bias_ref, *, rows):
    hd = HEAD_DIM
    nq = NA_QROWS * GRID_W
    nkeys = NA_KROWS * GRID_W
    nblk = rows // NA_QROWS
    unroll = math.gcd(nblk, NA_UNROLL)

    @pl.when(pl.program_id(1) == 0)
    def _():
        idx = _na_block_offsets(rows)
        for cls in range(3):
            for ri in range(NA_QROWS):
                for kp in range(NA_KROWS // 2):
                    d0, d1 = int(idx[cls, ri, 2 * kp]), int(idx[cls, ri, 2 * kp + 1])
                    if d0 == NA_MASKED and d1 == NA_MASKED:
                        tile = jnp.full((GRID_W, 2 * GRID_W), NEG_BIG, F32)
                    else:
                        tile = jnp.concatenate([t_ref[0, d0], t_ref[0, d1]], axis=1)
                    bias_ref[cls, ri * GRID_W:(ri + 1) * GRID_W,
                             kp * 2 * GRID_W:(kp + 1) * 2 * GRID_W] = tile

    va_ref[:, :hd] = v_ref[0, 0]
    va_ref[:, hd:] = jnp.ones((va_ref.shape[0], hd), BF16)

    def block_scores(rb):
        qoff = pl.multiple_of(rb * nq, nq)
        start = jnp.clip(rb * NA_QROWS - NA_WIN_ROWS // 2, 0, rows - NA_KROWS)
        koff = pl.multiple_of(start * GRID_W, GRID_W)
        cls = jnp.where(rb == 0, 0, jnp.where(rb == nblk - 1, 2, 1))
        q = q_ref[0, 0, pl.ds(qoff, nq), :]
        k = k_ref[0, 0, pl.ds(koff, nkeys), :]
        s = lax.dot_general(q, k, (((1,), (1,)), ((), ())), preferred_element_type=F32)
        return qoff, koff, s + bias_ref[cls]

    def body(it, carry):
        items = [block_scores(it * unroll + u) for u in range(unroll)]
        for qoff, koff, s in items:
            m = jnp.max(s, axis=1, keepdims=True)
            p = jnp.exp2(s - m).astype(BF16)
            oa = jnp.dot(p, va_ref[pl.ds(koff, nkeys), :], preferred_element_type=F32)
            o_ref[0, pl.ds(qoff, nq), :] = (oa[:, :hd] / oa[:, hd:hd + 1]).astype(BF16)
        return carry

    lax.fori_loop(0, nblk // unroll, body, 0)


def _na_attn(qkv, t):
    bsz, _, s, hd = qkv.shape
    rows = s // GRID_W
    nq = NA_QROWS * GRID_W
    nkeys = NA_KROWS * GRID_W
    assert rows >= NA_KROWS and rows % NA_QROWS == 0 and NA_KROWS % 2 == 0
    return pl.pallas_call(
        functools.partial(_na_attn_kernel, rows=rows),
        out_shape=jax.ShapeDtypeStruct((bsz, s, N_HEADS * hd), BF16),
        grid=(N_HEADS, bsz),
        in_specs=[
            pl.BlockSpec((1, 1, s, hd), lambda h, b: (b, h, 0, 0)),
            pl.BlockSpec((1, 1, s, hd), lambda h, b: (b, N_HEADS + h, 0, 0)),
            pl.BlockSpec((1, 1, s, hd), lambda h, b: (b, 2 * N_HEADS + h, 0, 0)),
            pl.BlockSpec((1, NA_MASKED + 1, GRID_W, GRID_W), lambda h, b: (h, 0, 0, 0)),
        ],
        out_specs=pl.BlockSpec((1, s, hd), lambda h, b: (b, 0, h)),
        scratch_shapes=[pltpu.VMEM((s, 2 * hd), BF16), pltpu.VMEM((3, nq, nkeys), F32)],
        compiler_params=pltpu.CompilerParams(
            dimension_semantics=("parallel", "arbitrary"), vmem_limit_bytes=VMEM_LIMIT),
        name="na_attn",
    )(qkv, qkv, qkv, t)


def _na_bias_tiles(rpb):
    h = rpb.shape[0]
    c = np.arange(GRID_W)
    col_start = np.clip(c - NA_WIN_COLS // 2, 0, GRID_W - NA_WIN_COLS)
    kc = np.arange(GRID_W)
    col_ok = (kc[None, :] >= col_start[:, None]) & (kc[None, :] < col_start[:, None] + NA_WIN_COLS)
    pad = GRID_W - NA_WIN_COLS
    padded = jnp.pad(rpb * LOG2E, ((0, 0), (0, 0), (pad, pad)))
    t = jnp.stack([padded[:, :, GRID_W - 1 - ci: 2 * GRID_W - 1 - ci] for ci in range(GRID_W)], axis=2)
    t = jnp.where(jnp.asarray(col_ok)[None, None], t, NEG_BIG)
    return jnp.concatenate([t, jnp.full((h, 1, GRID_W, GRID_W), NEG_BIG, F32)], axis=1)


def _rope_tables(seq):
    t = np.arange(seq)
    row = (t // GRID_W).astype(np.float32)
    col = (t % GRID_W).astype(np.float32)
    axis_dim = HEAD_DIM // 2
    inv_freq = np.float32(ROPE_THETA) ** (-np.arange(0, axis_dim, 2, dtype=np.float32) / np.float32(axis_dim))
    ar = row[:, None] * inv_freq
    ac = col[:, None] * inv_freq
    cos = np.concatenate([np.cos(ar), np.cos(ac), np.cos(ar), np.cos(ac)], axis=1)
    sin = np.concatenate([-np.sin(ar), -np.sin(ac), np.sin(ar), np.sin(ac)], axis=1)
    return jnp.asarray(cos, F32), jnp.asarray(sin, F32)


def _rope_layout(a, n_heads):
    nf = HEAD_DIM // 4
    lead = a.shape[:-1]
    n = n_heads * HEAD_DIM
    p = a[..., :n].reshape(*lead, n_heads, 2, 2, nf)
    p = jnp.swapaxes(p, -3, -2).reshape(*lead, n)
    return jnp.concatenate([p, a[..., n:]], axis=-1)


def _gain_rows(g):
    g = _rope_layout(g, 1)
    return jnp.stack([g, jnp.roll(g, HEAD_DIM // 2)])


def kernel(x, ln_g, ln_b, ffn_w_in, ffn_w_out, ga_w_qkv, ga_q_norm, ga_k_norm, ga_w_o,
           na_w_qkv, na_rpb, na_w_o):
    bsz, s, d = x.shape
    m = bsz * s
    cos, sin = _rope_tables(s)

    def ln_p(i, k):
        return ln_g[i, k][None, :], ln_b[i, k][None, :]

    w_in = ffn_w_in.astype(BF16)
    w_out = ffn_w_out.astype(BF16)

    def ffn(h, i, k):
        return _ffn_ln(h, w_in, w_out, *ln_p(i, 2 * k), i, k)

    h = x.reshape(m, d)
    for i in range(DEPTH):
        h = ffn(h, i, 0)
        j = i // 2
        if i % 2 == 0:
            qg, kg = _gain_rows(ga_q_norm[j]), _gain_rows(ga_k_norm[j])
            w_qkv = _rope_layout(ga_w_qkv[j], N_HEADS + GA_KV_HEADS).astype(BF16)
            q, k, vt = _gqa_qkv(h.reshape(bsz, s, d), w_qkv, cos, sin, qg, kg)
            o = _gqa_flash(q, k, vt, qg, kg)
            w_o = ga_w_o[j]
        else:
            qkv = _na_qkv(h.reshape(bsz, s, d), na_w_qkv[j].astype(BF16))
            o = _na_attn(qkv, _na_bias_tiles(na_rpb[j]))
            w_o = na_w_o[j]
        h = _proj_ln(o.reshape(m, d), h, w_o.astype(BF16), *ln_p(i, 1))
        h = ffn(h, i, 1)
    return h.reshape(bsz, s, d)
```

```python
import functools
import math

import jax
import jax.numpy as jnp
import numpy as np
from jax import lax
from jax.experimental import pallas as pl
from jax.experimental.pallas import tpu as pltpu

F32 = jnp.float32
BF16 = jnp.bfloat16

GRID_W = 64
HEAD_DIM = 128
N_HEADS = 16
GA_KV_HEADS = 4
GA_GROUPS = N_HEADS // GA_KV_HEADS
ROPE_THETA = 10000.0
NA_WIN_ROWS = 8
NA_WIN_COLS = 16
DEPTH = 2
DEEPNORM_ALPHA = (2 * DEPTH) ** 0.25
LN_EPS = 1e-5
QK_EPS = 1e-6
ATTN_SCALE = HEAD_DIM ** -0.5
LOG2E = math.log2(math.e)
SCORE_SCALE = ATTN_SCALE * LOG2E
NEG_BIG = -1e30
VT_ROWS = HEAD_DIM + 16
BOUND_SLACK = 1.0 + 2.0 ** -5
STABILISER_MAX = 48.0
FLASH_UNROLL = 16

VMEM_LIMIT = 56 * 1024 * 1024
VMEM_LIMIT_FFN = 62 * 1024 * 1024

NA_QROWS = 4
NA_KROWS = NA_QROWS + NA_WIN_ROWS
NA_MASKED = 2 * NA_WIN_ROWS - 1
NA_UNROLL = 32


def _layer_norm(z, g, b, z_scale=1.0):
    mu = jnp.mean(z, axis=-1, keepdims=True)
    zc = z - mu
    var = jnp.mean(zc * zc, axis=-1, keepdims=True)
    return zc * lax.rsqrt(var + (z_scale * z_scale) * LN_EPS) * g + b


def _ffn_ln_kernel(x_ref, wg_ref, wu_ref, wo_ref, g_ref, b_ref, o_ref, xb_ref, *, sub, sub_edge):
    j = pl.program_id(1)
    last = pl.num_programs(1) - 1
    tm = xb_ref.shape[0]

    def swiglu_out(xb):
        gate = jnp.dot(xb, wg_ref[...], preferred_element_type=F32)
        up = jnp.dot(xb, wu_ref[...], preferred_element_type=F32)
        h = (gate * jax.nn.sigmoid(gate) * up).astype(BF16)
        return jnp.dot(h, wo_ref[...], preferred_element_type=F32)

    @pl.when(j == 0)
    def _():
        for r in range(0, tm, sub):
            xb = x_ref[r:r + sub, :].astype(BF16)
            xb_ref[r:r + sub, :] = xb
            o_ref[r:r + sub, :] = swiglu_out(xb)

    @pl.when(jnp.logical_and(j > 0, j < last))
    def _():
        for r in range(0, tm, sub):
            o_ref[r:r + sub, :] += swiglu_out(xb_ref[r:r + sub, :])

    @pl.when(j == last)
    def _():
        def total(r):
            return o_ref[r:r + sub_edge, :] + swiglu_out(xb_ref[r:r + sub_edge, :])

        y_next = total(0)
        for r in range(0, tm, sub_edge):
            y = y_next
            if r + sub_edge < tm:
                y_next = total(r + sub_edge)
            z2 = (2.0 * DEEPNORM_ALPHA) * x_ref[r:r + sub_edge, :] + y
            o_ref[r:r + sub_edge, :] = _layer_norm(z2, g_ref[...], b_ref[...], z_scale=2.0)


def _ffn_ln(x2, w_in, w_out, g, b, layer, which, *, tm=1024, tf=512, sub=1024, sub_edge=256):
    m, d = x2.shape
    f = w_out.shape[2]
    nf = f // tf
    assert m % tm == 0 and tm % sub == 0 and tm % sub_edge == 0 and f % tf == 0 and nf >= 2
    return pl.pallas_call(
        functools.partial(_ffn_ln_kernel, sub=sub, sub_edge=sub_edge),
        out_shape=jax.ShapeDtypeStruct((m, d), F32),
        grid=(m // tm, nf),
        in_specs=[
            pl.BlockSpec((tm, d), lambda i, j: (i, 0)),
            pl.BlockSpec((None, None, d, tf), lambda i, j: (layer, which, 0, j)),
            pl.BlockSpec((None, None, d, tf), lambda i, j: (layer, which, 0, j + nf)),
            pl.BlockSpec((None, None, tf, d), lambda i, j: (layer, which, j, 0)),
            pl.BlockSpec((1, d), lambda i, j: (0, 0)),
            pl.BlockSpec((1, d), lambda i, j: (0, 0)),
        ],
        out_specs=pl.BlockSpec((tm, d), lambda i, j: (i, 0)),
        scratch_shapes=[pltpu.VMEM((tm, d), BF16)],
        compiler_params=pltpu.CompilerParams(
            dimension_semantics=("parallel", "arbitrary"), vmem_limit_bytes=VMEM_LIMIT_FFN),
        name="ffn_ln",
    )(x2, w_in, w_in, w_out, g, b)


def _proj_ln_kernel(a_ref, x_ref, w_ref, g_ref, b_ref, o_ref, *, sub):
    for r in range(0, a_ref.shape[0], sub):
        y = jnp.dot(a_ref[r:r + sub, :], w_ref[...], preferred_element_type=F32)
        z = DEEPNORM_ALPHA * x_ref[r:r + sub, :] + y
        o_ref[r:r + sub, :] = _layer_norm(z, g_ref[...], b_ref[...])


def _proj_ln(a2, x2, w, g, b, *, tm=1024, sub=128):
    m, d = x2.shape
    k = a2.shape[1]
    return pl.pallas_call(
        functools.partial(_proj_ln_kernel, sub=sub),
        out_shape=jax.ShapeDtypeStruct((m, d), F32),
        grid=(m // tm,),
        in_specs=[
            pl.BlockSpec((tm, k), lambda i: (i, 0)),
            pl.BlockSpec((tm, d), lambda i: (i, 0)),
            pl.BlockSpec((k, d), lambda i: (0, 0)),
            pl.BlockSpec((1, d), lambda i: (0, 0)),
            pl.BlockSpec((1, d), lambda i: (0, 0)),
        ],
        out_specs=pl.BlockSpec((tm, d), lambda i: (i, 0)),
        compiler_params=pltpu.CompilerParams(
            dimension_semantics=("parallel",), vmem_limit_bytes=VMEM_LIMIT),
        name="proj_ln",
    )(a2, x2, w, g, b)


def _score_bound(qg, kg):
    qmax = jnp.max(jnp.abs(qg), axis=1, keepdims=True)
    kmax = jnp.max(jnp.abs(kg), axis=1, keepdims=True)
    return (SCORE_SCALE * HEAD_DIM * BOUND_SLACK) * qmax * kmax


def _gqa_qkv_kernel(x_ref, w_ref, cos_ref, sin_ref, qg_ref, kg_ref, q_ref, k_ref, vt_ref,
                    *, heads_per_dot):
    tm = x_ref.shape[1]
    xb = x_ref[0].astype(BF16)
    lane0 = lax.broadcasted_iota(jnp.int32, (tm, HEAD_DIM), 1) == 0
    cos = cos_ref[...]
    sin = sin_ref[...]
    q_cos = cos * (qg_ref[0:1, :] * SCORE_SCALE)
    q_sin = sin * (qg_ref[1:2, :] * SCORE_SCALE)
    k_cos = cos * kg_ref[0:1, :]
    k_sin = sin * kg_ref[1:2, :]

    def norm_rope(yh, c, s):
        ms = jnp.mean(yh * yh, axis=-1, keepdims=True)
        return (yh * c + pltpu.roll(yh, HEAD_DIM // 2, 1) * s) * lax.rsqrt(ms + QK_EPS)

    q_hi = jnp.where(lane0, -_score_bound(qg_ref[0:1, :], kg_ref[0:1, :]), 0.0).astype(BF16)
    k_hi = jnp.where(lane0, 1.0, 0.0).astype(BF16)
    ones = jnp.ones((VT_ROWS - HEAD_DIM, tm), BF16)

    n_all = N_HEADS + 2 * GA_KV_HEADS

    def project(h0):
        c0 = h0 * HEAD_DIM
        return jnp.dot(xb, w_ref[:, c0:c0 + heads_per_dot * HEAD_DIM], preferred_element_type=F32)

    y_next = project(0)
    for h0 in range(0, n_all, heads_per_dot):
        y = y_next
        if h0 + heads_per_dot < n_all:
            y_next = project(h0 + heads_per_dot)
        for t in range(heads_per_dot):
            h = h0 + t
            yh = y[:, t * HEAD_DIM:(t + 1) * HEAD_DIM]
            if h < N_HEADS:
                q_ref[0, h, :, :HEAD_DIM] = norm_rope(yh, q_cos, q_sin).astype(BF16)
                q_ref[0, h, :, HEAD_DIM:] = q_hi
            elif h < N_HEADS + GA_KV_HEADS:
                k_ref[0, h - N_HEADS, :, :HEAD_DIM] = norm_rope(yh, k_cos, k_sin).astype(BF16)
                k_ref[0, h - N_HEADS, :, HEAD_DIM:] = k_hi
            else:
                hv = h - N_HEADS - GA_KV_HEADS
                vt_ref[0, hv, 0, :HEAD_DIM, :] = yh.T.astype(BF16)
                vt_ref[0, hv, 0, HEAD_DIM:, :] = ones


def _gqa_qkv(x3, w, cos, sin, qg, kg, *, tm=512, heads_per_dot=4):
    bsz, s, d = x3.shape
    n = w.shape[1]
    hd = HEAD_DIM
    return pl.pallas_call(
        functools.partial(_gqa_qkv_kernel, heads_per_dot=heads_per_dot),
        out_shape=(
            jax.ShapeDtypeStruct((bsz, N_HEADS, s, 2 * hd), BF16),
            jax.ShapeDtypeStruct((bsz, GA_KV_HEADS, s, 2 * hd), BF16),
            jax.ShapeDtypeStruct((bsz, GA_KV_HEADS, s // tm, VT_ROWS, tm), BF16),
        ),
        grid=(bsz, s // tm),
        in_specs=[
            pl.BlockSpec((1, tm, d), lambda b, i: (b, i, 0)),
            pl.BlockSpec((d, n), lambda b, i: (0, 0)),
            pl.BlockSpec((tm, hd), lambda b, i: (i, 0)),
            pl.BlockSpec((tm, hd), lambda b, i: (i, 0)),
            pl.BlockSpec((2, hd), lambda b, i: (0, 0)),
            pl.BlockSpec((2, hd), lambda b, i: (0, 0)),
        ],
        out_specs=(
            pl.BlockSpec((1, N_HEADS, tm, 2 * hd), lambda b, i: (b, 0, i, 0)),
            pl.BlockSpec((1, GA_KV_HEADS, tm, 2 * hd), lambda b, i: (b, 0, i, 0)),
            pl.BlockSpec((1, GA_KV_HEADS, 1, VT_ROWS, tm), lambda b, i: (b, 0, i, 0, 0)),
        ),
        compiler_params=pltpu.CompilerParams(
            dimension_semantics=("parallel", "parallel"), vmem_limit_bytes=VMEM_LIMIT),
        name="gqa_qkv",
    )(x3, w, cos, sin, qg, kg)


def _gqa_flash_kernel(q_ref, k_ref, vt_ref, qg_ref, kg_ref, o_ref, qt_ref, mx_ref, acc_ref, st_ref,
                      *, tq, tk):
    hd = HEAD_DIM
    nk = k_ref.shape[2] // tk
    unroll = math.gcd(nk, FLASH_UNROLL)
    exact_max = jnp.max(_score_bound(qg_ref[0:1, :], kg_ref[0:1, :])) > STABILISER_MAX
    mx_ref[...] = jnp.zeros(mx_ref.shape, BF16)

    @pl.when(exact_max)
    def _():
        row0 = lax.broadcasted_iota(jnp.int32, (16, tq), 0) == 0
        for h in range(GA_GROUPS):
            def max_body(i, m):
                off = pl.multiple_of(i * tk, tk)
                s = lax.dot_general(k_ref[0, 0, pl.ds(off, tk), :hd], q_ref[0, h, :, :hd],
                                    (((1,), (1,)), ((), ())), preferred_element_type=F32)
                return jnp.maximum(m, jnp.max(s, axis=0, keepdims=True))

            m = lax.fori_loop(0, nk, max_body, jnp.full((1, tq), NEG_BIG, F32))
            mx_ref[h] = jnp.where(row0, -m, 0.0).astype(BF16)

    for h in range(GA_GROUPS):
        qt = q_ref[0, h].T
        qt_ref[h] = qt
        qt_ref[h, hd:hd + 16, :] = jnp.where(exact_max, mx_ref[h], qt[hd:hd + 16, :])
    acc_ref[...] = jnp.zeros(acc_ref.shape, F32)

    def scores(i, h):
        off = pl.multiple_of(i * tk, tk)
        return jnp.dot(k_ref[0, 0, pl.ds(off, tk), :], qt_ref[h],
                       preferred_element_type=F32)

    st_ref[...] = scores(0, 0)

    def body(it, carry):
        st = st_ref[...]
        for u in range(unroll):
            i = it * unroll + u
            vt = vt_ref[0, 0, i]
            for h in range(GA_GROUPS):
                if h + 1 < GA_GROUPS:
                    st_next = scores(i, h + 1)
                else:
                    st_next = scores(jnp.minimum(i + 1, nk - 1), 0)
                p = jnp.exp2(st).astype(BF16)
                acc_ref[h] += jnp.dot(vt, p, preferred_element_type=F32)
                st = st_next
        st_ref[...] = st
        return carry

    lax.fori_loop(0, nk // unroll, body, 0)
    for h in range(GA_GROUPS):
        out_t = acc_ref[h, :hd, :] / acc_ref[h, hd:hd + 1, :]
        o_ref[0, :, h * hd:(h + 1) * hd] = out_t.T.astype(BF16)


def _gqa_flash(q, k, vt, qg, kg, *, tq=512):
    bsz, _, s, hd2 = q.shape
    hd = hd2 // 2
    nk, tk = vt.shape[2], vt.shape[4]
    assert s % tq == 0
    return pl.pallas_call(
        functools.partial(_gqa_flash_kernel, tq=tq, tk=tk),
        out_shape=jax.ShapeDtypeStruct((bsz, s, N_HEADS * hd), BF16),
        grid=(bsz, GA_KV_HEADS, s // tq),
        in_specs=[
            pl.BlockSpec((1, GA_GROUPS, tq, hd2), lambda b, g, i: (b, g, i, 0)),
            pl.BlockSpec((1, 1, s, hd2), lambda b, g, i: (b, g, 0, 0)),
            pl.BlockSpec((1, 1, nk, VT_ROWS, tk), lambda b, g, i: (b, g, 0, 0, 0)),
            pl.BlockSpec((2, hd), lambda b, g, i: (0, 0)),
            pl.BlockSpec((2, hd), lambda b, g, i: (0, 0)),
        ],
        out_specs=pl.BlockSpec((1, tq, GA_GROUPS * hd), lambda b, g, i: (b, i, g)),
        scratch_shapes=[pltpu.VMEM((GA_GROUPS, hd2, tq), BF16), pltpu.VMEM((GA_GROUPS, 16, tq), BF16),
                        pltpu.VMEM((GA_GROUPS, VT_ROWS, tq), F32), pltpu.VMEM((tk, tq), F32)],
        compiler_params=pltpu.CompilerParams(
            dimension_semantics=("parallel", "parallel", "parallel"), vmem_limit_bytes=VMEM_LIMIT),
        name="gqa_flash",
    )(q, k, vt, qg, kg)


def _na_qkv_kernel(x_ref, w_ref, o_ref, *, q_tiles):
    xb = x_ref[0].astype(BF16)
    y = jnp.dot(xb, w_ref[...], preferred_element_type=F32)
    y = y * jnp.where(pl.program_id(2) < q_tiles, SCORE_SCALE, 1.0)
    for t in range(o_ref.shape[1]):
        o_ref[0, t] = y[:, t * HEAD_DIM:(t + 1) * HEAD_DIM].astype(BF16)


def _na_qkv(x3, w, *, tm=1024, tn=2048):
    bsz, s, d = x3.shape
    n = w.shape[1]
    hd = HEAD_DIM
    return pl.pallas_call(
        functools.partial(_na_qkv_kernel, q_tiles=N_HEADS * hd // tn),
        out_shape=jax.ShapeDtypeStruct((bsz, n // hd, s, hd), BF16),
        grid=(bsz, s // tm, n // tn),
        in_specs=[
            pl.BlockSpec((1, tm, d), lambda b, i, j: (b, i, 0)),
            pl.BlockSpec((d, tn), lambda b, i, j: (0, j)),
        ],
        out_specs=pl.BlockSpec((1, tn // hd, tm, hd), lambda b, i, j: (b, j, i, 0)),
        compiler_params=pltpu.CompilerParams(
            dimension_semantics=("parallel", "parallel", "arbitrary"), vmem_limit_bytes=VMEM_LIMIT),
        name="na_qkv",
    )(x3, w)


def _na_block_offsets(rows):
    nblk = rows // NA_QROWS
    idx = np.full((3, NA_QROWS, NA_KROWS), NA_MASKED, np.int32)
    for cls, rb in enumerate((0, 1, nblk - 1)):
        r0 = rb * NA_QROWS
        start = int(np.clip(r0 - NA_WIN_ROWS // 2, 0, rows - NA_KROWS))
        for ri in range(NA_QROWS):
            r = r0 + ri
            row_start = int(np.clip(r - NA_WIN_ROWS // 2, 0, rows - NA_WIN_ROWS))
            for ki in range(NA_KROWS):
                kr = start + ki
                if row_start <= kr < row_start + NA_WIN_ROWS:
                    idx[cls, ri, ki] = kr - r + (NA_WIN_ROWS - 1)
    return idx


def _na_attn_kernel(q_ref, k_ref, v_ref, t_ref, o_ref, va_ref, bias_ref, *, rows):
    hd = HEAD_DIM
    nq = NA_QROWS * GRID_W
    nkeys = NA_KROWS * GRID_W
    nblk = rows // NA_QROWS
    unroll = math.gcd(nblk, NA_UNROLL)

    @pl.when(pl.program_id(1) == 0)
    def _():
        idx = _na_block_offsets(rows)
        for cls in range(3):
            for ri in range(NA_QROWS):
                for kp in range(NA_KROWS // 2):
                    d0, d1 = int(idx[cls, ri, 2 * kp]), int(idx[cls, ri, 2 * kp + 1])
                    if d0 == NA_MASKED and d1 == NA_MASKED:
                        tile = jnp.full((GRID_W, 2 * GRID_W), NEG_BIG, F32)
                    else:
                        tile = jnp.concatenate([t_ref[0, d0], t_ref[0, d1]], axis=1)
                    bias_ref[cls, ri * GRID_W:(ri + 1) * GRID_W,
                             kp * 2 * GRID_W:(kp + 1) * 2 * GRID_W] = tile

    va_ref[:, :hd] = v_ref[0, 0]
    va_ref[:, hd:] = jnp.ones((va_ref.shape[0], hd), BF16)

    def block_scores(rb):
        qoff = pl.multiple_of(rb * nq, nq)
        start = jnp.clip(rb * NA_QROWS - NA_WIN_ROWS // 2, 0, rows - NA_KROWS)
        koff = pl.multiple_of(start * GRID_W, GRID_W)
        cls = jnp.where(rb == 0, 0, jnp.where(rb == nblk - 1, 2, 1))
        q = q_ref[0, 0, pl.ds(qoff, nq), :]
        k = k_ref[0, 0, pl.ds(koff, nkeys), :]
        s = lax.dot_general(q, k, (((1,), (1,)), ((), ())), preferred_element_type=F32)
        return qoff, koff, s + bias_ref[cls]

    def body(it, carry):
        items = [block_scores(it * unroll + u) for u in range(unroll)]
        for qoff, koff, s in items:
            m = jnp.max(s, axis=1, keepdims=True)
            p = jnp.exp2(s - m).astype(BF16)
            oa = jnp.dot(p, va_ref[pl.ds(koff, nkeys), :], preferred_element_type=F32)
            o_ref[0, pl.ds(qoff, nq), :] = (oa[:, :hd] / oa[:, hd:hd + 1]).astype(BF16)
        return carry

    lax.fori_loop(0, nblk // unroll, body, 0)


def _na_attn(qkv, t):
    bsz, _, s, hd = qkv.shape
    rows = s // GRID_W
    nq = NA_QROWS * GRID_W
    nkeys = NA_KROWS * GRID_W
    assert rows >= NA_KROWS and rows % NA_QROWS == 0 and NA_KROWS % 2 == 0
    return pl.pallas_call(
        functools.partial(_na_attn_kernel, rows=rows),
        out_shape=jax.ShapeDtypeStruct((bsz, s, N_HEADS * hd), BF16),
        grid=(N_HEADS, bsz),
        in_specs=[
            pl.BlockSpec((1, 1, s, hd), lambda h, b: (b, h, 0, 0)),
            pl.BlockSpec((1, 1, s, hd), lambda h, b: (b, N_HEADS + h, 0, 0)),
            pl.BlockSpec((1, 1, s, hd), lambda h, b: (b, 2 * N_HEADS + h, 0, 0)),
            pl.BlockSpec((1, NA_MASKED + 1, GRID_W, GRID_W), lambda h, b: (h, 0, 0, 0)),
        ],
        out_specs=pl.BlockSpec((1, s, hd), lambda h, b: (b, 0, h)),
        scratch_shapes=[pltpu.VMEM((s, 2 * hd), BF16), pltpu.VMEM((3, nq, nkeys), F32)],
        compiler_params=pltpu.CompilerParams(
            dimension_semantics=("parallel", "arbitrary"), vmem_limit_bytes=VMEM_LIMIT),
        name="na_attn",
    )(qkv, qkv, qkv, t)


def _na_bias_tiles(rpb):
    h = rpb.shape[0]
    c = np.arange(GRID_W)
    col_start = np.clip(c - NA_WIN_COLS // 2, 0, GRID_W - NA_WIN_COLS)
    kc = np.arange(GRID_W)
    col_ok = (kc[None, :] >= col_start[:, None]) & (kc[None, :] < col_start[:, None] + NA_WIN_COLS)
    pad = GRID_W - NA_WIN_COLS
    padded = jnp.pad(rpb * LOG2E, ((0, 0), (0, 0), (pad, pad)))
    t = jnp.stack([padded[:, :, GRID_W - 1 - ci: 2 * GRID_W - 1 - ci] for ci in range(GRID_W)], axis=2)
    t = jnp.where(jnp.asarray(col_ok)[None, None], t, NEG_BIG)
    return jnp.concatenate([t, jnp.full((h, 1, GRID_W, GRID_W), NEG_BIG, F32)], axis=1)


def _rope_tables(seq):
    t = np.arange(seq)
    row = (t // GRID_W).astype(np.float32)
    col = (t % GRID_W).astype(np.float32)
    axis_dim = HEAD_DIM // 2
    inv_freq = np.float32(ROPE_THETA) ** (-np.arange(0, axis_dim, 2, dtype=np.float32) / np.float32(axis_dim))
    ar = row[:, None] * inv_freq
    ac = col[:, None] * inv_freq
    cos = np.concatenate([np.cos(ar), np.cos(ac), np.cos(ar), np.cos(ac)], axis=1)
    sin = np.concatenate([-np.sin(ar), -np.sin(ac), np.sin(ar), np.sin(ac)], axis=1)
    return jnp.asarray(cos, F32), jnp.asarray(sin, F32)


def _rope_layout(a, n_heads):
    nf = HEAD_DIM // 4
    lead = a.shape[:-1]
    n = n_heads * HEAD_DIM
    p = a[..., :n].reshape(*lead, n_heads, 2, 2, nf)
    p = jnp.swapaxes(p, -3, -2).reshape(*lead, n)
    return jnp.concatenate([p, a[..., n:]], axis=-1)


def _gain_rows(g):
    g = _rope_layout(g, 1)
    return jnp.stack([g, jnp.roll(g, HEAD_DIM // 2)])


def kernel(x, ln_g, ln_b, ffn_w_in, ffn_w_out, ga_w_qkv, ga_q_norm, ga_k_norm, ga_w_o,
           na_w_qkv, na_rpb, na_w_o):
    bsz, s, d = x.shape
    m = bsz * s
    cos, sin = _rope_tables(s)

    def ln_p(i, k):
        return ln_g[i, k][None, :], ln_b[i, k][None, :]

    w_in = ffn_w_in.astype(BF16)
    w_out = ffn_w_out.astype(BF16)

    def ffn(h, i, k):
        return _ffn_ln(h, w_in, w_out, *ln_p(i, 2 * k), i, k)

    h = x.reshape(m, d)
    for i in range(DEPTH):
        h = ffn(h, i, 0)
        j = i // 2
        if i % 2 == 0:
            qg, kg = _gain_rows(ga_q_norm[j]), _gain_rows(ga_k_norm[j])
            w_qkv = _rope_layout(ga_w_qkv[j], N_HEADS + GA_KV_HEADS).astype(BF16)
            q, k, vt = _gqa_qkv(h.reshape(bsz, s, d), w_qkv, cos, sin, qg, kg)
            o = _gqa_flash(q, k, vt, qg, kg)
            w_o = ga_w_o[j]
        else:
            qkv = _na_qkv(h.reshape(bsz, s, d), na_w_qkv[j].astype(BF16))
            o = _na_attn(qkv, _na_bias_tiles(na_rpb[j]))
            w_o = na_w_o[j]
        h = _proj_ln(o.reshape(m, d), h, w_o.astype(BF16), *ln_p(i, 1))
        h = ffn(h, i, 1)
    return h.reshape(bsz, s, d)
```
